```python
import math
import jax, jax.numpy as jnp
from jax import lax
import numpy as np

D_MODEL = 1024
BATCH = 8
SEQ = 8192
DEPTH = 2

N_A_LAYERS = DEPTH // 2
N_B_LAYERS = DEPTH - N_A_LAYERS

A_HEADS = 8
A_HEAD_DIM = 64
A_INNER = A_HEADS * 2 * A_HEAD_DIM
B_HEADS = 16
B_KV_HEADS = 2
B_HEAD_DIM = 64
B_GROUP = B_HEADS // B_KV_HEADS
B_INNER = B_HEADS * B_HEAD_DIM
WINDOW = 128
Q_BLOCK = 128
EPS = 1e-6
ADA_STD = 0.02

kernel_name = "yoco_diffattn_swa_sink_hybrid"


def rms_norm(x, g):
    x32 = x.astype(jnp.float32)
    y = x32 * lax.rsqrt(jnp.mean(x32 * x32, axis=-1, keepdims=True) + EPS)
    return (y * g.astype(jnp.float32)).astype(x.dtype)


def alibi_slopes(n_heads):
    return jnp.asarray(np.array([2.0 ** (-8.0 * (h + 1) / n_heads) for h in range(n_heads)], dtype=np.float32))


def ada_modulation(c, w, b, n_chunks):
    m = jax.nn.silu(c) @ w + b
    return jnp.split(m[:, None, :], n_chunks, axis=-1)


def diff_causal_attention(q, k, v, lam, slopes):
    bsz, seq = q.shape[0], q.shape[1]
    nb = seq // Q_BLOCK
    scale = A_HEAD_DIM ** -0.5
    qb = jnp.moveaxis(q.reshape(bsz, nb, Q_BLOCK, 2 * A_HEADS, A_HEAD_DIM), 1, 0)
    kpos = jnp.arange(seq)
    sub_slopes = jnp.repeat(slopes, 2)[:, None, None]
    lam = lam.astype(jnp.float32)

    def block(args):
        qi, blk = args
        qpos = blk * Q_BLOCK + jnp.arange(Q_BLOCK)
        dist = (qpos[:, None] - kpos[None, :]).astype(jnp.float32)
        s = jnp.einsum('bqhd,bkhd->bhqk', qi, k).astype(jnp.float32) * scale - sub_slopes * dist
        s = jnp.where(dist >= 0, s, -jnp.inf)
        p = jax.nn.softmax(s, axis=-1).reshape(bsz, A_HEADS, 2, Q_BLOCK, seq)
        a = (p[:, :, 0] - lam * p[:, :, 1]).astype(v.dtype)
        return jnp.einsum('bhqk,bkhe->bqhe', a, v)

    o = lax.map(block, (qb, jnp.arange(nb)))
    return jnp.moveaxis(o, 0, 1).reshape(bsz, seq, A_HEADS, 2 * A_HEAD_DIM)


def sliding_window_sink_attention(q, k, v, sinks, slopes):
    bsz, seq = q.shape[0], q.shape[1]
    nb = seq // WINDOW
    scale = B_HEAD_DIM ** -0.5
    qb = q.reshape(bsz, nb, WINDOW, B_KV_HEADS, B_GROUP, B_HEAD_DIM)

    def band(t):
        tb = t.reshape(bsz, nb, WINDOW, B_KV_HEADS, B_HEAD_DIM)
        prev = jnp.concatenate([jnp.zeros_like(tb[:, :1]), tb[:, :-1]], axis=1)
        return jnp.concatenate([prev, tb], axis=2)

    kw, vw = band(k), band(v)
    s = jnp.einsum('bnqkgd,bnskd->bnkgqs', qb, kw).astype(jnp.float32) * scale
    dist = jnp.arange(WINDOW)[:, None] - jnp.arange(2 * WINDOW)[None, :] + WINDOW
    kpos = jnp.arange(nb)[:, None] * WINDOW + jnp.arange(2 * WINDOW)[None, :] - WINDOW
    valid = (dist >= 0) & (dist < WINDOW) & (kpos[:, None, :] >= 0)
    s = s - slopes.reshape(B_KV_HEADS, B_GROUP, 1, 1) * dist.astype(jnp.float32)
    s = jnp.where(valid[:, None, None], s, -jnp.inf)
    sink = sinks.astype(jnp.float32).reshape(1, 1, B_KV_HEADS, B_GROUP, 1)
    m = jnp.maximum(jnp.max(s, axis=-1), sink)
    e = jnp.exp(s - m[..., None])
    p = e / (jnp.sum(e, axis=-1) + jnp.exp(sink - m))[..., None]
    o = jnp.einsum('bnkgqs,bnskd->bnqkgd', p.astype(v.dtype), vw)
    return o.reshape(bsz, seq, B_HEADS, B_HEAD_DIM)


def setup_inputs(seed: int = 0) -> dict:
    key = jax.random.key(seed)
    ks = iter(jax.random.split(key, 32))
    nrm = lambda shape, std: std * jax.random.normal(next(ks), shape, dtype=jnp.float32)
    gain = lambda shape: 1.0 + nrm(shape, 0.02)
    d = D_MODEL
    kv_width = 2 * B_KV_HEADS * B_HEAD_DIM
    return {
        "x": nrm((BATCH, SEQ, d), 1.0),
        "c": nrm((BATCH, d), 1.0),
        "a_norm_g": gain((N_A_LAYERS, d)),
        "a_ada_w": nrm((N_A_LAYERS, d, 3 * d), ADA_STD),
        "a_ada_b": nrm((N_A_LAYERS, 3 * d), 0.02),
        "a_w_in": nrm((N_A_LAYERS, d, 4 * A_INNER), d ** -0.5),
        "a_q_norm_g": gain((N_A_LAYERS, A_HEAD_DIM)),
        "a_k_norm_g": gain((N_A_LAYERS, A_HEAD_DIM)),
        "a_lambda_q1": nrm((N_A_LAYERS, A_HEAD_DIM), 0.1),
        "a_lambda_k1": nrm((N_A_LAYERS, A_HEAD_DIM), 0.1),
        "a_lambda_q2": nrm((N_A_LAYERS, A_HEAD_DIM), 0.1),
        "a_lambda_k2": nrm((N_A_LAYERS, A_HEAD_DIM), 0.1),
        "a_subln_g": gain((N_A_LAYERS, 2 * A_HEAD_DIM)),
        "a_w_out": nrm((N_A_LAYERS, A_INNER, d), A_INNER ** -0.5),
        "kv_norm_g": gain((d,)),
        "kv_ada_w": nrm((d, 2 * d), ADA_STD),
        "kv_ada_b": nrm((2 * d,), 0.02),
        "w_kv": nrm((d, kv_width), d ** -0.5),
        "kv_k_norm_g": gain((B_HEAD_DIM,)),
        "b_norm_g": gain((N_B_LAYERS, d)),
        "b_ada_w": nrm((N_B_LAYERS, d, 3 * d), ADA_STD),
        "b_ada_b": nrm((N_B_LAYERS, 3 * d), 0.02),
        "b_w_in": nrm((N_B_LAYERS, d, 2 * B_INNER), d ** -0.5),
        "b_q_norm_g": gain((N_B_LAYERS, B_HEAD_DIM)),
        "b_sinks": nrm((N_B_LAYERS, B_HEADS), 0.5),
        "b_w_out": nrm((N_B_LAYERS, B_INNER, d), B_INNER ** -0.5),
    }


def reference(x, c, a_norm_g, a_ada_w, a_ada_b, a_w_in, a_q_norm_g, a_k_norm_g,
              a_lambda_q1, a_lambda_k1, a_lambda_q2, a_lambda_k2, a_subln_g, a_w_out,
              kv_norm_g, kv_ada_w, kv_ada_b, w_kv, kv_k_norm_g,
              b_norm_g, b_ada_w, b_ada_b, b_w_in, b_q_norm_g, b_sinks, b_w_out):
    bsz, seq, _ = x.shape
    slopes_a = alibi_slopes(A_HEADS)
    slopes_b = alibi_slopes(B_HEADS)
    h = x
    k_s = v_s = None
    for layer in range(DEPTH):
        if layer < N_A_LAYERS:
            l = layer
            shift, scale, gate = ada_modulation(c, a_ada_w[l], a_ada_b[l], 3)
            u = rms_norm(h, a_norm_g[l]) * (1 + scale) + shift
            q, k, v, z = jnp.split(u @ a_w_in[l], 4, axis=-1)
            q = rms_norm(q.reshape(bsz, seq, 2 * A_HEADS, A_HEAD_DIM), a_q_norm_g[l])
            k = rms_norm(k.reshape(bsz, seq, 2 * A_HEADS, A_HEAD_DIM), a_k_norm_g[l])
            v = v.reshape(bsz, seq, A_HEADS, 2 * A_HEAD_DIM)
            lambda_init = 0.8 - 0.6 * math.exp(-0.3 * l)
            lam = (jnp.exp(jnp.sum(a_lambda_q1[l].astype(jnp.float32) * a_lambda_k1[l].astype(jnp.float32)))
                   - jnp.exp(jnp.sum(a_lambda_q2[l].astype(jnp.float32) * a_lambda_k2[l].astype(jnp.float32)))
                   + lambda_init)
            o = diff_causal_attention(q, k, v, lam, slopes_a)
            o = rms_norm(o, a_subln_g[l]) * (1.0 - lambda_init)
            o = o.reshape(bsz, seq, A_INNER) * jax.nn.silu(z)
            h = h + gate * (o @ a_w_out[l])
            if layer == N_A_LAYERS - 1:
                kv_shift, kv_scale = ada_modulation(c, kv_ada_w, kv_ada_b, 2)
                ukv = rms_norm(h, kv_norm_g) * (1 + kv_scale) + kv_shift
                k_s, v_s = jnp.split(ukv @ w_kv, 2, axis=-1)
                k_s = rms_norm(k_s.reshape(bsz, seq, B_KV_HEADS, B_HEAD_DIM), kv_k_norm_g)
                v_s = v_s.reshape(bsz, seq, B_KV_HEADS, B_HEAD_DIM)
        else:
            l = layer - N_A_LAYERS
            shift, scale, gate = ada_modulation(c, b_ada_w[l], b_ada_b[l], 3)
            u = rms_norm(h, b_norm_g[l]) * (1 + scale) + shift
            q, z = jnp.split(u @ b_w_in[l], 2, axis=-1)
            q = rms_norm(q.reshape(bsz, seq, B_HEADS, B_HEAD_DIM), b_q_norm_g[l])
            o = sliding_window_sink_attention(q, k_s, v_s, b_sinks[l], slopes_b)
            o = o.reshape(bsz, seq, B_INNER) * jax.nn.silu(z)
            h = h + gate * (o @ b_w_out[l])
    return h
```

```python
import functools
import math

import numpy as np
import jax
import jax.numpy as jnp
from jax import lax
from jax.experimental import pallas as pl
from jax.experimental.pallas import tpu as pltpu

EPS = 1e-6
LANES = 128
A_HEADS = 8
A_HEAD_DIM = 64
B_HEADS = 16
B_KV_HEADS = 2
B_GROUP = B_HEADS // B_KV_HEADS
B_HEAD_DIM = 64
WINDOW = 128
NEG = -1e30
V_ROWS = 2 * A_HEAD_DIM + 16

TM = 512
TQ = 512
TK = 512
VMEM_LIMIT = 56 * 1024 * 1024

F32 = jnp.float32
BF16 = jnp.bfloat16
NT_DIMS = (((1,), (1,)), ((), ()))


def _alibi_slopes(n_heads):
    return [2.0 ** (-8.0 * (h + 1) / n_heads) for h in range(n_heads)]


def _silu(x):
    return x * (1.0 / (1.0 + jnp.exp(-x)))


def _const_spec(shape):
    zeros = (0,) * len(shape)
    return pl.BlockSpec(shape, lambda *_: zeros, pipeline_mode=pl.Buffered(1))


def _params():
    return pltpu.CompilerParams(dimension_semantics=("arbitrary",) * 2, vmem_limit_bytes=VMEM_LIMIT)


def _mod_kernel(c_ref, aw_ref, kw_ref, bw_ref, ab_ref, kb_ref, bb_ref, o_ref):
    j = pl.program_id(0)
    sc = _silu(c_ref[...])

    def emit(w_ref, b_ref):
        o_ref[...] = jnp.dot(sc, w_ref[...], precision=lax.Precision.HIGHEST,
                             preferred_element_type=F32) + b_ref[...]

    @pl.when(j < 3)
    def _():
        emit(aw_ref, ab_ref)

    @pl.when((j >= 3) & (j < 5))
    def _():
        emit(kw_ref, kb_ref)

    @pl.when(j >= 5)
    def _():
        emit(bw_ref, bb_ref)


def _modulation(c, a_w, a_b, kv_w, kv_b, b_w, b_b):
    bsz, d = c.shape
    wspec = lambda f: pl.BlockSpec((d, d), lambda j: (0, f(j)))
    bspec = lambda f: pl.BlockSpec((1, d), lambda j: (0, f(j)))
    fa = lambda j: jnp.minimum(j, 2)
    fk = lambda j: jnp.clip(j - 3, 0, 1)
    fb = lambda j: jnp.clip(j - 5, 0, 2)
    out = pl.pallas_call(
        _mod_kernel,
        grid=(8,),
        in_specs=[pl.BlockSpec((bsz, d), lambda j: (0, 0)),
                  wspec(fa), wspec(fk), wspec(fb), bspec(fa), bspec(fk), bspec(fb)],
        out_specs=pl.BlockSpec((bsz, d), lambda j: (0, j)),
        out_shape=jax.ShapeDtypeStruct((bsz, 8 * d), F32),
        compiler_params=pltpu.CompilerParams(dimension_semantics=("arbitrary",),
                                             vmem_limit_bytes=VMEM_LIMIT),
        name="ada_modulation",
    )(c, a_w, kv_w, b_w, a_b.reshape(1, -1), kv_b.reshape(1, -1), b_b.reshape(1, -1))
    return out.reshape(bsz, 8, d)


def _modulated_norm(x, g, scale, shift):
    ms = jnp.mean(x * x, axis=-1, keepdims=True)
    return (x * lax.rsqrt(ms + EPS) * g) * (1.0 + scale) + shift


def _head_norm(slab, gain, head_dim):
    ss = jnp.sum(slab * slab, axis=-1, keepdims=True) * (1.0 / head_dim)
    return slab * lax.rsqrt(ss + EPS) * gain


def _proj_a_kernel(x_ref, mod_ref, g_ref, wqk_ref, wz_ref, wvt_ref, gq_ref, gk_ref,
                   qp_ref, kp_ref, vt_ref, z_ref, *, slopes):
    tm = x_ref.shape[1]
    u = _modulated_norm(x_ref[0], g_ref[...], mod_ref[0, 1:2, :], mod_ref[0, 0:1, :]).astype(BF16)

    lane = lax.broadcasted_iota(jnp.int32, (tm, LANES), 1)
    col_hi = lane == A_HEAD_DIM
    col_lo = lane == A_HEAD_DIM + 1
    pos = lax.broadcasted_iota(jnp.int32, (tm, LANES), 0).astype(F32)
    gq = gq_ref[...]
    gk = gk_ref[...]
    n_sub = 2 * A_HEADS
    for h in range(A_HEADS):
        qq = jnp.dot(u, wqk_ref[:, h * 2 * LANES:(h + 1) * 2 * LANES], preferred_element_type=F32)
        kk = jnp.dot(u, wqk_ref[:, (n_sub + 2 * h) * LANES:(n_sub + 2 * h + 2) * LANES],
                     preferred_element_type=F32)
        bias = pos * slopes[h]
        bias_hi = bias.astype(BF16).astype(F32)
        bias_lo = bias - bias_hi
        for s in range(2):
            qn = _head_norm(qq[:, s * LANES:(s + 1) * LANES], gq, A_HEAD_DIM)
            qn = jnp.where(col_hi | col_lo, 1.0, qn)
            qp_ref[0, 2 * h + s] = qn.astype(BF16)
            kn = _head_norm(kk[:, s * LANES:(s + 1) * LANES], gk, A_HEAD_DIM)
            kn = jnp.where(col_hi, bias_hi, jnp.where(col_lo, bias_lo, kn))
            kp_ref[0, 2 * h + s] = kn.astype(BF16)

    z_ref[0] = jnp.dot(u, wz_ref[...], preferred_element_type=F32).astype(BF16)

    v_t = lax.dot_general(wvt_ref[...], u, NT_DIMS, preferred_element_type=F32)
    pad_rows = V_ROWS - 2 * A_HEAD_DIM
    ones_blk = jnp.where(lax.broadcasted_iota(jnp.int32, (pad_rows, tm), 0) == 0, 1.0, 0.0).astype(BF16)
    vd = 2 * A_HEAD_DIM
    for h in range(A_HEADS):
        vt_ref[0, h, 0, 0:vd, :] = v_t[h * vd:(h + 1) * vd].astype(BF16)
        vt_ref[0, h, 0, vd:V_ROWS, :] = ones_blk


def _proj_a(x, mod, g, wqk, wz, wvt, gq, gk):
    bsz, seq, d = x.shape
    nt = seq // TM
    n_sub = 2 * A_HEADS
    kern = functools.partial(_proj_a_kernel, slopes=_alibi_slopes(A_HEADS))
    return pl.pallas_call(
        kern,
        grid=(bsz, nt),
        in_specs=[pl.BlockSpec((1, TM, d), lambda b, t: (b, t, 0)),
                  pl.BlockSpec((1, 8, d), lambda b, t: (b, 0, 0)),
                  _const_spec((1, d)),
                  _const_spec(wqk.shape), _const_spec(wz.shape), _const_spec(wvt.shape),
                  _const_spec((1, LANES)), _const_spec((1, LANES))],
        out_specs=[pl.BlockSpec((1, n_sub, TM, LANES), lambda b, t: (b, 0, t, 0)),
                   pl.BlockSpec((1, n_sub, TM, LANES), lambda b, t: (b, 0, t, 0)),
                   pl.BlockSpec((1, A_HEADS, 1, V_ROWS, TM), lambda b, t: (b, 0, t, 0, 0)),
                   pl.BlockSpec((1, TM, d), lambda b, t: (b, t, 0))],
        out_shape=[jax.ShapeDtypeStruct((bsz, n_sub, seq, LANES), BF16),
                   jax.ShapeDtypeStruct((bsz, n_sub, seq, LANES), BF16),
                   jax.ShapeDtypeStruct((bsz, A_HEADS, nt, V_ROWS, TM), BF16),
                   jax.ShapeDtypeStruct((bsz, seq, d), BF16)],
        compiler_params=_params(),
        name="proj_a",
    )(x, mod, g, wqk, wz, wvt, gq, gk)


def _attn_a_kernel(qp_ref, kp_ref, vt_ref, z_ref, lq1_ref, lk1_ref, lq2_ref, lk2_ref, gs_ref,
                   og_ref, acc_ref, m_ref, *, slopes, lambda_init):
    h = pl.program_id(1)
    i = pl.program_id(2)
    tq = qp_ref.shape[2]
    tk = kp_ref.shape[3]
    vd = 2 * A_HEAD_DIM

    slope = jnp.float32(slopes[0])
    for hh in range(1, A_HEADS):
        slope = jnp.where(h == hh, jnp.float32(slopes[hh]), slope)

    acc_ref[...] = jnp.zeros(acc_ref.shape, F32)
    m_ref[...] = jnp.full(m_ref.shape, NEG, F32)

    def kv_tile(j, masked):
        c = slope * (j * tk - i * tq).astype(F32)
        v_t = vt_ref[0, 0, j]
        for sub in range(2):
            s = lax.dot_general(kp_ref[0, sub, j], qp_ref[0, sub], NT_DIMS,
                                preferred_element_type=F32)
            if masked:
                krow = lax.broadcasted_iota(jnp.int32, (tk, tq), 0)
                qcol = lax.broadcasted_iota(jnp.int32, (tk, tq), 1)
                s = jnp.where(krow <= qcol, s, NEG)
            m_old = m_ref[sub]
            m_new = jnp.maximum(m_old, jnp.max(s, axis=0, keepdims=True) + c)
            alpha = jnp.exp(m_old - m_new)
            p = jnp.exp(s - (m_new - c)).astype(BF16)
            pv = jnp.dot(v_t, p, preferred_element_type=F32)
            acc_ref[sub] = acc_ref[sub] * alpha + pv
            m_ref[sub] = m_new

    def body(j, carry):
        kv_tile(j, False)
        return carry

    lax.fori_loop(0, i, body, 0)
    kv_tile(i, True)

    lam = (jnp.exp(jnp.sum(lq1_ref[...] * lk1_ref[...], axis=-1, keepdims=True))
           - jnp.exp(jnp.sum(lq2_ref[...] * lk2_ref[...], axis=-1, keepdims=True))
           + lambda_init)
    a1 = acc_ref[0]
    a2 = acc_ref[1]
    o_t = (a1[0:vd] * (1.0 / a1[vd:vd + 1]) - lam * (a2[0:vd] * (1.0 / a2[vd:vd + 1])))
    ms = jnp.mean(o_t * o_t, axis=0, keepdims=True)
    o = (o_t * lax.rsqrt(ms + EPS)).T
    o = o * gs_ref[...] * (1.0 - lambda_init)
    og_ref[0] = (o * _silu(z_ref[0].astype(F32))).astype(BF16)


def _attn_a(qp, kp, vt, z, lq1, lk1, lq2, lk2, gs, lambda_init):
    bsz, n_sub, seq, _ = qp.shape
    nk = seq // TK
    nq = seq // TQ
    d = z.shape[-1]
    vd = 2 * A_HEAD_DIM
    kp5 = kp.reshape(bsz, n_sub, nk, TK, LANES)
    kern = functools.partial(_attn_a_kernel, slopes=_alibi_slopes(A_HEADS), lambda_init=lambda_init)
    vec = lambda n: pl.BlockSpec((1, n), lambda b, h, i: (0, 0))
    return pl.pallas_call(
        kern,
        grid=(bsz, A_HEADS, nq),
        in_specs=[pl.BlockSpec((1, 2, TQ, LANES), lambda b, h, i: (b, h, i, 0)),
                  pl.BlockSpec((1, 2, nk, TK, LANES), lambda b, h, i: (b, h, 0, 0, 0)),
                  pl.BlockSpec((1, 1, nk, V_ROWS, TK), lambda b, h, i: (b, h, 0, 0, 0)),
                  pl.BlockSpec((1, TQ, vd), lambda b, h, i: (b, i, h)),
                  vec(A_HEAD_DIM), vec(A_HEAD_DIM), vec(A_HEAD_DIM), vec(A_HEAD_DIM), vec(vd)],
        out_specs=pl.BlockSpec((1, TQ, vd), lambda b, h, i: (b, i, h)),
        out_shape=jax.ShapeDtypeStruct((bsz, seq, d), BF16),
        scratch_shapes=[pltpu.VMEM((2, V_ROWS, TQ), F32), pltpu.VMEM((2, 1, TQ), F32)],
        compiler_params=pltpu.CompilerParams(dimension_semantics=("arbitrary",) * 3,
                                             vmem_limit_bytes=VMEM_LIMIT),
        name="attn_a",
    )(qp, kp5, vt, z, lq1, lk1, lq2, lk2, gs)


def _mid_kernel(x_ref, og_ref, mod_ref, wo_ref, gkv_ref, gb_ref, wk_ref, wvt_ref, wqb_ref, wzb_ref,
                gks_ref, gqb_ref, h1_ref, qb_ref, ks_ref, vst_ref, zb_ref):
    gate = mod_ref[0, 2:3, :]
    h1 = x_ref[0] + gate * jnp.dot(og_ref[0], wo_ref[...], preferred_element_type=F32)
    h1_ref[0] = h1

    ms = jnp.mean(h1 * h1, axis=-1, keepdims=True)
    r = h1 * lax.rsqrt(ms + EPS)
    ukv = ((r * gkv_ref[...]) * (1.0 + mod_ref[0, 4:5, :]) + mod_ref[0, 3:4, :]).astype(BF16)
    ub = ((r * gb_ref[...]) * (1.0 + mod_ref[0, 6:7, :]) + mod_ref[0, 5:6, :]).astype(BF16)

    kk = jnp.dot(ukv, wk_ref[...], preferred_element_type=F32)
    gks = gks_ref[...]
    for g in range(B_KV_HEADS):
        ks_ref[0, g] = _head_norm(kk[:, g * LANES:(g + 1) * LANES], gks, B_HEAD_DIM).astype(BF16)
    v_t = lax.dot_general(wvt_ref[...], ukv, NT_DIMS, preferred_element_type=F32)
    for g in range(B_KV_HEADS):
        vst_ref[0, g] = v_t[g * B_HEAD_DIM:(g + 1) * B_HEAD_DIM].astype(BF16)

    gqb = gqb_ref[...]
    for pr in range(B_HEADS // 2):
        qq = jnp.dot(ub, wqb_ref[:, pr * 2 * LANES:(pr + 1) * 2 * LANES], preferred_element_type=F32)
        for s in range(2):
            qb_ref[0, 2 * pr + s] = _head_norm(qq[:, s * LANES:(s + 1) * LANES], gqb,
                                               B_HEAD_DIM).astype(BF16)
    zb_ref[0] = jnp.dot(ub, wzb_ref[...], preferred_element_type=F32).astype(BF16)


def _mid(x, og, mod, wo, gkv, gb, wk, wvt, wqb, wzb, gks, gqb):
    bsz, seq, d = x.shape
    nt = seq // TM
    tok = lambda: pl.BlockSpec((1, TM, d), lambda b, t: (b, t, 0))
    return pl.pallas_call(
        _mid_kernel,
        grid=(bsz, nt),
        in_specs=[tok(), tok(),
                  pl.BlockSpec((1, 8, d), lambda b, t: (b, 0, 0)),
                  _const_spec(wo.shape), _const_spec((1, d)), _const_spec((1, d)),
                  _const_spec(wk.shape), _const_spec(wvt.shape), _const_spec(wqb.shape),
                  _const_spec(wzb.shape), _const_spec((1, LANES)), _const_spec((1, LANES))],
        out_specs=[tok(),
                   pl.BlockSpec((1, B_HEADS, TM, LANES), lambda b, t: (b, 0, t, 0)),
                   pl.BlockSpec((1, B_KV_HEADS, TM, LANES), lambda b, t: (b, 0, t, 0)),
                   pl.BlockSpec((1, B_KV_HEADS, B_HEAD_DIM, TM), lambda b, t: (b, 0, 0, t)),
                   tok()],
        out_shape=[jax.ShapeDtypeStruct((bsz, seq, d), F32),
                   jax.ShapeDtypeStruct((bsz, B_HEADS, seq, LANES), BF16),
                   jax.ShapeDtypeStruct((bsz, B_KV_HEADS, seq, LANES), BF16),
                   jax.ShapeDtypeStruct((bsz, B_KV_HEADS, B_HEAD_DIM, seq), BF16),
                   jax.ShapeDtypeStruct((bsz, seq, d), BF16)],
        compiler_params=_params(),
        name="mid_proj",
    )(x, og, mod, wo, gkv, gb, wk, wvt, wqb, wzb, gks, gqb)


def _final_kernel(h1_ref, qb_ref, ksc_ref, ksp_ref, vtc_ref, vtp_ref, zb_ref, wo_ref, mod_ref,
                  bias_ref, sink_ref, out_ref, o_scr):
    t = pl.program_id(1)
    tm = h1_ref.shape[1]
    w = WINDOW
    first_tile = t == 0
    prev_rows = lax.broadcasted_iota(jnp.int32, (2 * w, B_GROUP * w), 0) < w
    for g in range(B_KV_HEADS):
        bias = bias_ref[g]
        sink = sink_ref[g]
        for nb in range(tm // w):
            if nb == 0:
                kband = jnp.concatenate([ksp_ref[0, g], ksc_ref[0, g, 0:w, :]], axis=0)
                vband = jnp.concatenate([vtp_ref[0, g], vtc_ref[0, g, :, 0:w]], axis=1)
            else:
                kband = ksc_ref[0, g, (nb - 1) * w:(nb + 1) * w, :]
                vband = vtc_ref[0, g, :, (nb - 1) * w:(nb + 1) * w]
            qs = jnp.concatenate([qb_ref[0, B_GROUP * g + hh, nb * w:(nb + 1) * w, :]
                                  for hh in range(B_GROUP)], axis=0)
            s = lax.dot_general(kband, qs, NT_DIMS, preferred_element_type=F32) + bias
            if nb == 0:
                s = jnp.where(first_tile & prev_rows, NEG, s)
            m = jnp.maximum(jnp.max(s, axis=0, keepdims=True), sink)
            e = jnp.exp(s - m)
            den = jnp.sum(e, axis=0, keepdims=True) + jnp.exp(sink - m)
            o_t = jnp.dot(vband, e.astype(BF16), preferred_element_type=F32) * (1.0 / den)
            for pr in range(B_GROUP // 2):
                pair = jnp.concatenate([o_t[:, (2 * pr) * w:(2 * pr + 1) * w],
                                        o_t[:, (2 * pr + 1) * w:(2 * pr + 2) * w]], axis=0)
                col = (B_GROUP * g + 2 * pr) * B_HEAD_DIM
                o_scr[nb * w:(nb + 1) * w, col:col + 2 * B_HEAD_DIM] = pair.T
    og = (o_scr[...] * _silu(zb_ref[0].astype(F32))).astype(BF16)
    out_ref[0] = h1_ref[0] + mod_ref[0, 7:8, :] * jnp.dot(og, wo_ref[...], preferred_element_type=F32)


def _final(h1, qb, ks, vst, zb, wo, mod, bias, sink):
    bsz, seq, d = h1.shape
    nt = seq // TM
    nblk = TM // WINDOW
    tok = lambda: pl.BlockSpec((1, TM, d), lambda b, t: (b, t, 0))
    prev = lambda t: jnp.maximum(t * nblk - 1, 0)
    return pl.pallas_call(
        _final_kernel,
        grid=(bsz, nt),
        in_specs=[tok(),
                  pl.BlockSpec((1, B_HEADS, TM, LANES), lambda b, t: (b, 0, t, 0)),
                  pl.BlockSpec((1, B_KV_HEADS, TM, LANES), lambda b, t: (b, 0, t, 0)),
                  pl.BlockSpec((1, B_KV_HEADS, WINDOW, LANES), lambda b, t: (b, 0, prev(t), 0)),
                  pl.BlockSpec((1, B_KV_HEADS, B_HEAD_DIM, TM), lambda b, t: (b, 0, 0, t)),
                  pl.BlockSpec((1, B_KV_HEADS, B_HEAD_DIM, WINDOW), lambda b, t: (b, 0, 0, prev(t))),
                  tok(),
                  _const_spec(wo.shape),
                  pl.BlockSpec((1, 8, d), lambda b, t: (b, 0, 0)),
                  _const_spec(bias.shape), _const_spec(sink.shape)],
        out_specs=tok(),
        out_shape=jax.ShapeDtypeStruct((bsz, seq, d), F32),
        scratch_shapes=[pltpu.VMEM((TM, d), F32)],
        compiler_params=_params(),
        name="attn_b_out",
    )(h1, qb, ks, ks, vst, vst, zb, wo, mod, bias, sink)


def _pad_heads(w, n_heads, head_dim):
    d = w.shape[0]
    w = w.reshape(d, n_heads, head_dim)
    w = jnp.pad(w, ((0, 0), (0, 0), (0, LANES - head_dim)))
    return w.reshape(d, n_heads * LANES)


def _pad_gain(g, head_dim, scale):
    return jnp.pad(g.astype(F32) * scale, (0, LANES - head_dim)).reshape(1, LANES)


def _swa_bias_table():
    slopes = np.asarray(_alibi_slopes(B_HEADS), dtype=np.float32)
    k = np.arange(2 * WINDOW)[:, None]
    q = np.arange(WINDOW)[None, :]
    dist = q + WINDOW - k
    valid = (dist >= 0) & (dist < WINDOW)
    tab = np.empty((B_KV_HEADS, 2 * WINDOW, B_GROUP * WINDOW), np.float32)
    for g in range(B_KV_HEADS):
        for hh in range(B_GROUP):
            blk = np.where(valid, -slopes[B_GROUP * g + hh] * dist.astype(np.float32), np.float32(NEG))
            tab[g, :, hh * WINDOW:(hh + 1) * WINDOW] = blk
    return jnp.asarray(tab)


def kernel(x, c, a_norm_g, a_ada_w, a_ada_b, a_w_in, a_q_norm_g, a_k_norm_g, a_lambda_q1, a_lambda_k1,
           a_lambda_q2, a_lambda_k2, a_subln_g, a_w_out, kv_norm_g, kv_ada_w, kv_ada_b, w_kv, kv_k_norm_g,
           b_norm_g, b_ada_w, b_ada_b, b_w_in, b_q_norm_g, b_sinks, b_w_out):
    bsz, seq, d = x.shape
    assert a_norm_g.shape[0] == 1 and b_norm_g.shape[0] == 1, "one layer of each mixer"
    assert TM == TK and seq % TM == 0 and seq % TQ == 0 and TQ == TK and TM % WINDOW == 0
    a_inner = A_HEADS * 2 * A_HEAD_DIM
    b_inner = B_HEADS * B_HEAD_DIM
    n_sub = 2 * A_HEADS

    mod = _modulation(c, a_ada_w[0], a_ada_b[0], kv_ada_w, kv_ada_b, b_ada_w[0], b_ada_b[0])

    w_in = a_w_in[0]
    wq = _pad_heads(w_in[:, 0:a_inner], n_sub, A_HEAD_DIM)
    wk = _pad_heads(w_in[:, a_inner:2 * a_inner], n_sub, A_HEAD_DIM)
    wqk = jnp.concatenate([wq, wk], axis=1).astype(BF16)
    wvt = w_in[:, 2 * a_inner:3 * a_inner].T.astype(BF16)
    wz = w_in[:, 3 * a_inner:].astype(BF16)
    q_scale = A_HEAD_DIM ** -0.5
    gq = _pad_gain(a_q_norm_g[0], A_HEAD_DIM, q_scale)
    gk = _pad_gain(a_k_norm_g[0], A_HEAD_DIM, 1.0)

    qp, kp, vt, z = _proj_a(x, mod, a_norm_g[0].reshape(1, d), wqk, wz, wvt, gq, gk)

    lambda_init = 0.8 - 0.6 * math.exp(-0.3 * 0)
    row = lambda v: v.astype(F32).reshape(1, -1)
    og = _attn_a(qp, kp, vt, z, row(a_lambda_q1[0]), row(a_lambda_k1[0]), row(a_lambda_q2[0]),
                 row(a_lambda_k2[0]), row(a_subln_g[0]), lambda_init)

    kv_w = B_KV_HEADS * B_HEAD_DIM
    wks = _pad_heads(w_kv[:, 0:kv_w], B_KV_HEADS, B_HEAD_DIM).astype(BF16)
    wvst = w_kv[:, kv_w:].T.astype(BF16)
    wqb = _pad_heads(b_w_in[0][:, 0:b_inner], B_HEADS, B_HEAD_DIM).astype(BF16)
    wzb = b_w_in[0][:, b_inner:].astype(BF16)
    gks = _pad_gain(kv_k_norm_g, B_HEAD_DIM, 1.0)
    gqb = _pad_gain(b_q_norm_g[0], B_HEAD_DIM, B_HEAD_DIM ** -0.5)

    h1, qb, ks, vst, zb = _mid(x, og, mod, a_w_out[0].astype(BF16), kv_norm_g.reshape(1, d),
                               b_norm_g[0].reshape(1, d), wks, wvst, wqb, wzb, gks, gqb)

    sink = jnp.repeat(b_sinks[0].astype(F32).reshape(B_KV_HEADS, B_GROUP), WINDOW, axis=1)
    sink = sink.reshape(B_KV_HEADS, 1, B_GROUP * WINDOW)
    return _final(h1, qb, ks, vst, zb, b_w_out[0].astype(BF16), mod, _swa_bias_table(), sink)
```

```python
import functools
import math

import numpy as np
import jax
import jax.numpy as jnp
from jax import lax
from jax.experimental import pallas as pl
from jax.experimental.pallas import tpu as pltpu

EPS = 1e-6
LANES = 128
A_HEADS = 8
A_HEAD_DIM = 64
B_HEADS = 16
B_KV_HEADS = 2
B_GROUP = B_HEADS // B_KV_HEADS
B_HEAD_DIM = 64
WINDOW = 128
NEG = -1e30
V_ROWS = 2 * A_HEAD_DIM + 16

TM = 512
TQ = 1024
TK = 512
VMEM_LIMIT = 56 * 1024 * 1024

LOG2E = 1.4426950408889634
EXP2_ZERO = 152.0
BOUND_MAX = 50.0
BOUND_SLACK = 1.01

F32 = jnp.float32
BF16 = jnp.bfloat16
NT_DIMS = (((1,), (1,)), ((), ()))


def _alibi_slopes(n_heads):
    return [2.0 ** (-8.0 * (h + 1) / n_heads) for h in range(n_heads)]


def _alibi_slopes_log2(n_heads):
    return [s * LOG2E for s in _alibi_slopes(n_heads)]


def _silu(x):
    return x * (1.0 / (1.0 + jnp.exp(-x)))


def _const_spec(shape):
    zeros = (0,) * len(shape)
    return pl.BlockSpec(shape, lambda *_: zeros, pipeline_mode=pl.Buffered(1))


def _params():
    return pltpu.CompilerParams(dimension_semantics=("arbitrary",) * 2, vmem_limit_bytes=VMEM_LIMIT)


def _mod_kernel(c_ref, aw_ref, kw_ref, bw_ref, ab_ref, kb_ref, bb_ref, o_ref):
    j = pl.program_id(0)
    sc = _silu(c_ref[...])

    def emit(w_ref, b_ref):
        o_ref[...] = jnp.dot(sc, w_ref[...], precision=lax.Precision.HIGHEST,
                             preferred_element_type=F32) + b_ref[...]

    @pl.when(j < 3)
    def _():
        emit(aw_ref, ab_ref)

    @pl.when((j >= 3) & (j < 5))
    def _():
        emit(kw_ref, kb_ref)

    @pl.when(j >= 5)
    def _():
        emit(bw_ref, bb_ref)


def _modulation(c, a_w, a_b, kv_w, kv_b, b_w, b_b):
    bsz, d = c.shape
    wspec = lambda f: pl.BlockSpec((d, d), lambda j: (0, f(j)))
    bspec = lambda f: pl.BlockSpec((1, d), lambda j: (0, f(j)))
    fa = lambda j: jnp.minimum(j, 2)
    fk = lambda j: jnp.clip(j - 3, 0, 1)
    fb = lambda j: jnp.clip(j - 5, 0, 2)
    out = pl.pallas_call(
        _mod_kernel,
        grid=(8,),
        in_specs=[pl.BlockSpec((bsz, d), lambda j: (0, 0)),
                  wspec(fa), wspec(fk), wspec(fb), bspec(fa), bspec(fk), bspec(fb)],
        out_specs=pl.BlockSpec((bsz, d), lambda j: (0, j)),
        out_shape=jax.ShapeDtypeStruct((bsz, 8 * d), F32),
        compiler_params=pltpu.CompilerParams(dimension_semantics=("arbitrary",),
                                             vmem_limit_bytes=VMEM_LIMIT),
        name="ada_modulation",
    )(c, a_w, kv_w, b_w, a_b.reshape(1, -1), kv_b.reshape(1, -1), b_b.reshape(1, -1))
    return out.reshape(bsz, 8, d)


def _modulated_norm(x, g, scale, shift):
    ms = jnp.mean(x * x, axis=-1, keepdims=True)
    return (x * lax.rsqrt(ms + EPS) * g) * (1.0 + scale) + shift


def _head_norm(slab, gain, head_dim):
    ss = jnp.sum(slab * slab, axis=-1, keepdims=True) * (1.0 / head_dim)
    return slab * lax.rsqrt(ss + EPS) * gain


def _proj_a_kernel(x_ref, mod_ref, g_ref, wqk_ref, wz_ref, wvt_ref, gq_ref, gk_ref,
                   qp_ref, kp_ref, vt_ref, z_ref, *, slopes):
    tm = x_ref.shape[1]
    u = _modulated_norm(x_ref[0], g_ref[...], mod_ref[0, 1:2, :], mod_ref[0, 0:1, :]).astype(BF16)

    lane = lax.broadcasted_iota(jnp.int32, (tm, LANES), 1)
    col_hi = lane == A_HEAD_DIM
    col_lo = lane == A_HEAD_DIM + 1
    pos = lax.broadcasted_iota(jnp.int32, (tm, LANES), 0).astype(F32)
    gq = gq_ref[...]
    gk = gk_ref[...]
    n_sub = 2 * A_HEADS
    for h in range(A_HEADS):
        qq = jnp.dot(u, wqk_ref[:, h * 2 * LANES:(h + 1) * 2 * LANES], preferred_element_type=F32)
        kk = jnp.dot(u, wqk_ref[:, (n_sub + 2 * h) * LANES:(n_sub + 2 * h + 2) * LANES],
                     preferred_element_type=F32)
        bias = pos * slopes[h]
        bias_hi = bias.astype(BF16).astype(F32)
        bias_lo = bias - bias_hi
        for s in range(2):
            qn = _head_norm(qq[:, s * LANES:(s + 1) * LANES], gq, A_HEAD_DIM)
            qn = jnp.where(col_hi | col_lo, 1.0, qn)
            qp_ref[0, 2 * h + s] = qn.astype(BF16)
            kn = _head_norm(kk[:, s * LANES:(s + 1) * LANES], gk, A_HEAD_DIM)
            kn = jnp.where(col_hi, bias_hi, jnp.where(col_lo, bias_lo, kn))
            kp_ref[0, 2 * h + s] = kn.astype(BF16)

    z_ref[0] = jnp.dot(u, wz_ref[...], preferred_element_type=F32).astype(BF16)

    v_t = lax.dot_general(wvt_ref[...], u, NT_DIMS, preferred_element_type=F32)
    pad_rows = V_ROWS - 2 * A_HEAD_DIM
    ones_blk = jnp.where(lax.broadcasted_iota(jnp.int32, (pad_rows, tm), 0) == 0, 1.0, 0.0).astype(BF16)
    vd = 2 * A_HEAD_DIM
    for h in range(A_HEADS):
        vt_ref[0, h, 0, 0:vd, :] = v_t[h * vd:(h + 1) * vd].astype(BF16)
        vt_ref[0, h, 0, vd:V_ROWS, :] = ones_blk


def _proj_a(x, mod, g, wqk, wz, wvt, gq, gk):
    bsz, seq, d = x.shape
    nt = seq // TM
    n_sub = 2 * A_HEADS
    kern = functools.partial(_proj_a_kernel, slopes=_alibi_slopes_log2(A_HEADS))
    return pl.pallas_call(
        kern,
        grid=(bsz, nt),
        in_specs=[pl.BlockSpec((1, TM, d), lambda b, t: (b, t, 0)),
                  pl.BlockSpec((1, 8, d), lambda b, t: (b, 0, 0)),
                  _const_spec((1, d)),
                  _const_spec(wqk.shape), _const_spec(wz.shape), _const_spec(wvt.shape),
                  _const_spec((1, LANES)), _const_spec((1, LANES))],
        out_specs=[pl.BlockSpec((1, n_sub, TM, LANES), lambda b, t: (b, 0, t, 0)),
                   pl.BlockSpec((1, n_sub, TM, LANES), lambda b, t: (b, 0, t, 0)),
                   pl.BlockSpec((1, A_HEADS, 1, V_ROWS, TM), lambda b, t: (b, 0, t, 0, 0)),
                   pl.BlockSpec((1, TM, d), lambda b, t: (b, t, 0))],
        out_shape=[jax.ShapeDtypeStruct((bsz, n_sub, seq, LANES), BF16),
                   jax.ShapeDtypeStruct((bsz, n_sub, seq, LANES), BF16),
                   jax.ShapeDtypeStruct((bsz, A_HEADS, nt, V_ROWS, TM), BF16),
                   jax.ShapeDtypeStruct((bsz, seq, d), BF16)],
        compiler_params=_params(),
        name="proj_a",
    )(x, mod, g, wqk, wz, wvt, gq, gk)


def _attn_a_kernel(bound_ref, qp_ref, kp_ref, vt_ref, z_ref, lq1_ref, lk1_ref, lq2_ref, lk2_ref, gs_ref,
                   og_ref, acc_ref, m_ref, *, slopes2, lambda_init):
    h = pl.program_id(1)
    i = pl.program_id(2)
    tq = qp_ref.shape[2]
    tk = kp_ref.shape[3]
    vd = 2 * A_HEAD_DIM

    slope2 = jnp.float32(slopes2[0])
    zero_dist = jnp.int32(math.ceil(EXP2_ZERO / slopes2[0]))
    for hh in range(1, A_HEADS):
        slope2 = jnp.where(h == hh, jnp.float32(slopes2[hh]), slope2)
        zero_dist = jnp.where(h == hh, jnp.int32(math.ceil(EXP2_ZERO / slopes2[hh])), zero_dist)

    acc_ref[...] = jnp.zeros(acc_ref.shape, F32)
    bound = bound_ref[0]

    def tile_offset(j):
        return slope2 * (j * tk - i * tq).astype(F32)

    n_diag = tq // tk

    def causal_mask(r):
        krow = lax.broadcasted_iota(jnp.int32, (tk, tq), 0)
        qcol = lax.broadcasted_iota(jnp.int32, (tk, tq), 1)
        return krow - qcol <= -r * tk

    @pl.when(bound <= BOUND_MAX)
    def _():
        q_local = lax.broadcasted_iota(jnp.int32, (1, tq), 1).astype(F32)
        ref = bound + slope2 * q_local

        def kv_tile(j, diag=None):
            off = tile_offset(j) - ref
            v_t = vt_ref[0, 0, j]
            for sub in range(2):
                s = lax.dot_general(kp_ref[0, sub, j], qp_ref[0, sub], NT_DIMS,
                                    preferred_element_type=F32)
                e = jnp.exp2(s + off)
                if diag is not None:
                    e = jnp.where(causal_mask(diag), e, 0.0)
                acc_ref[sub] += jnp.dot(v_t, e.astype(BF16), preferred_element_type=F32)

        def body(j, carry):
            kv_tile(j)
            return carry

        first = jnp.maximum(i * tq - zero_dist + 1, 0) // tk
        lax.fori_loop(first, i * n_diag, body, 0)
        for r in range(n_diag):
            kv_tile(i * n_diag + r, diag=r)

    @pl.when(bound > BOUND_MAX)
    def _():
        m_ref[...] = jnp.full(m_ref.shape, NEG, F32)

        def kv_tile(j, diag=None):
            c = tile_offset(j)
            v_t = vt_ref[0, 0, j]
            for sub in range(2):
                s = lax.dot_general(kp_ref[0, sub, j], qp_ref[0, sub], NT_DIMS,
                                    preferred_element_type=F32)
                if diag is not None:
                    s = jnp.where(causal_mask(diag), s, NEG)
                m_old = m_ref[sub]
                m_new = jnp.maximum(m_old, jnp.max(s, axis=0, keepdims=True) + c)
                alpha = jnp.exp2(m_old - m_new)
                p = jnp.exp2(s - (m_new - c)).astype(BF16)
                pv = jnp.dot(v_t, p, preferred_element_type=F32)
                acc_ref[sub] = acc_ref[sub] * alpha + pv
                m_ref[sub] = m_new

        def body(j, carry):
            kv_tile(j)
            return carry

        lax.fori_loop(0, i * n_diag, body, 0)
        for r in range(n_diag):
            kv_tile(i * n_diag + r, diag=r)

    lam = (jnp.exp(jnp.sum(lq1_ref[...] * lk1_ref[...], axis=-1, keepdims=True))
           - jnp.exp(jnp.sum(lq2_ref[...] * lk2_ref[...], axis=-1, keepdims=True))
           + lambda_init)
    a1 = acc_ref[0]
    a2 = acc_ref[1]
    o_t = (a1[0:vd] * (1.0 / a1[vd:vd + 1]) - lam * (a2[0:vd] * (1.0 / a2[vd:vd + 1])))
    ms = jnp.mean(o_t * o_t, axis=0, keepdims=True)
    o = (o_t * lax.rsqrt(ms + EPS)).T
    o = o * gs_ref[...] * (1.0 - lambda_init)
    og_ref[0] = (o * _silu(z_ref[0].astype(F32))).astype(BF16)


def _attn_a(bound, qp, kp, vt, z, lq1, lk1, lq2, lk2, gs, lambda_init):
    bsz, n_sub, seq, _ = qp.shape
    nk = seq // TK
    nq = seq // TQ
    d = z.shape[-1]
    vd = 2 * A_HEAD_DIM
    kp5 = kp.reshape(bsz, n_sub, nk, TK, LANES)
    kern = functools.partial(_attn_a_kernel, slopes2=_alibi_slopes_log2(A_HEADS), lambda_init=lambda_init)
    vec = lambda n: pl.BlockSpec((1, n), lambda b, h, i: (0, 0))
    return pl.pallas_call(
        kern,
        grid=(bsz, A_HEADS, nq),
        in_specs=[pl.BlockSpec(memory_space=pltpu.SMEM),
                  pl.BlockSpec((1, 2, TQ, LANES), lambda b, h, i: (b, h, i, 0)),
                  pl.BlockSpec((1, 2, nk, TK, LANES), lambda b, h, i: (b, h, 0, 0, 0)),
                  pl.BlockSpec((1, 1, nk, V_ROWS, TK), lambda b, h, i: (b, h, 0, 0, 0)),
                  pl.BlockSpec((1, TQ, vd), lambda b, h, i: (b, i, h)),
                  vec(A_HEAD_DIM), vec(A_HEAD_DIM), vec(A_HEAD_DIM), vec(A_HEAD_DIM), vec(vd)],
        out_specs=pl.BlockSpec((1, TQ, vd), lambda b, h, i: (b, i, h)),
        out_shape=jax.ShapeDtypeStruct((bsz, seq, d), BF16),
        scratch_shapes=[pltpu.VMEM((2, V_ROWS, TQ), F32), pltpu.VMEM((2, 1, TQ), F32)],
        compiler_params=pltpu.CompilerParams(dimension_semantics=("arbitrary",) * 3,
                                             vmem_limit_bytes=VMEM_LIMIT),
        name="attn_a",
    )(bound, qp, kp5, vt, z, lq1, lk1, lq2, lk2, gs)


def _mid_kernel(x_ref, og_ref, mod_ref, wo_ref, gkv_ref, gb_ref, wk_ref, wvt_ref, wqb_ref, wzb_ref,
                gks_ref, gqb_ref, h1_ref, qb_ref, ks_ref, vst_ref, zb_ref):
    gate = mod_ref[0, 2:3, :]
    h1 = x_ref[0] + gate * jnp.dot(og_ref[0], wo_ref[...], preferred_element_type=F32)
    h1_ref[0] = h1

    ms = jnp.mean(h1 * h1, axis=-1, keepdims=True)
    r = h1 * lax.rsqrt(ms + EPS)
    ukv = ((r * gkv_ref[...]) * (1.0 + mod_ref[0, 4:5, :]) + mod_ref[0, 3:4, :]).astype(BF16)
    ub = ((r * gb_ref[...]) * (1.0 + mod_ref[0, 6:7, :]) + mod_ref[0, 5:6, :]).astype(BF16)

    kk = jnp.dot(ukv, wk_ref[...], preferred_element_type=F32)
    gks = gks_ref[...]
    for g in range(B_KV_HEADS):
        ks_ref[0, g] = _head_norm(kk[:, g * LANES:(g + 1) * LANES], gks, B_HEAD_DIM).astype(BF16)
    v_t = lax.dot_general(wvt_ref[...], ukv, NT_DIMS, preferred_element_type=F32)
    for g in range(B_KV_HEADS):
        vst_ref[0, g] = v_t[g * B_HEAD_DIM:(g + 1) * B_HEAD_DIM].astype(BF16)

    gqb = gqb_ref[...]
    for pr in range(B_HEADS // 2):
        qq = jnp.dot(ub, wqb_ref[:, pr * 2 * LANES:(pr + 1) * 2 * LANES], preferred_element_type=F32)
        for s in range(2):
            qb_ref[0, 2 * pr + s] = _head_norm(qq[:, s * LANES:(s + 1) * LANES], gqb,
                                               B_HEAD_DIM).astype(BF16)
    zb_ref[0] = jnp.dot(ub, wzb_ref[...], preferred_element_type=F32).astype(BF16)


def _mid(x, og, mod, wo, gkv, gb, wk, wvt, wqb, wzb, gks, gqb):
    bsz, seq, d = x.shape
    nt = seq // TM
    tok = lambda: pl.BlockSpec((1, TM, d), lambda b, t: (b, t, 0))
    return pl.pallas_call(
        _mid_kernel,
        grid=(bsz, nt),
        in_specs=[tok(), tok(),
                  pl.BlockSpec((1, 8, d), lambda b, t: (b, 0, 0)),
                  _const_spec(wo.shape), _const_spec((1, d)), _const_spec((1, d)),
                  _const_spec(wk.shape), _const_spec(wvt.shape), _const_spec(wqb.shape),
                  _const_spec(wzb.shape), _const_spec((1, LANES)), _const_spec((1, LANES))],
        out_specs=[tok(),
                   pl.BlockSpec((1, B_HEADS, TM, LANES), lambda b, t: (b, 0, t, 0)),
                   pl.BlockSpec((1, B_KV_HEADS, TM, LANES), lambda b, t: (b, 0, t, 0)),
                   pl.BlockSpec((1, B_KV_HEADS, B_HEAD_DIM, TM), lambda b, t: (b, 0, 0, t)),
                   tok()],
        out_shape=[jax.ShapeDtypeStruct((bsz, seq, d), F32),
                   jax.ShapeDtypeStruct((bsz, B_HEADS, seq, LANES), BF16),
                   jax.ShapeDtypeStruct((bsz, B_KV_HEADS, seq, LANES), BF16),
                   jax.ShapeDtypeStruct((bsz, B_KV_HEADS, B_HEAD_DIM, seq), BF16),
                   jax.ShapeDtypeStruct((bsz, seq, d), BF16)],
        compiler_params=_params(),
        name="mid_proj",
    )(x, og, mod, wo, gkv, gb, wk, wvt, wqb, wzb, gks, gqb)


def _final_kernel(h1_ref, qb_ref, ksc_ref, ksp_ref, vtc_ref, vtp_ref, zb_ref, wo_ref, mod_ref,
                  bias_ref, sink_ref, out_ref, o_scr):
    t = pl.program_id(1)
    tm = h1_ref.shape[1]
    w = WINDOW
    first_tile = t == 0
    prev_rows = lax.broadcasted_iota(jnp.int32, (2 * w, B_GROUP * w), 0) < w
    for g in range(B_KV_HEADS):
        bias = bias_ref[g]
        sink = sink_ref[g]
        for nb in range(tm // w):
            if nb == 0:
                kband = jnp.concatenate([ksp_ref[0, g], ksc_ref[0, g, 0:w, :]], axis=0)
                vband = jnp.concatenate([vtp_ref[0, g], vtc_ref[0, g, :, 0:w]], axis=1)
            else:
                kband = ksc_ref[0, g, (nb - 1) * w:(nb + 1) * w, :]
                vband = vtc_ref[0, g, :, (nb - 1) * w:(nb + 1) * w]
            qs = jnp.concatenate([qb_ref[0, B_GROUP * g + hh, nb * w:(nb + 1) * w, :]
                                  for hh in range(B_GROUP)], axis=0)
            s = lax.dot_general(kband, qs, NT_DIMS, preferred_element_type=F32) + bias
            if nb == 0:
                s = jnp.where(first_tile & prev_rows, NEG, s)
            m = jnp.maximum(jnp.max(s, axis=0, keepdims=True), sink)
            e = jnp.exp(s - m)
            den = jnp.sum(e, axis=0, keepdims=True) + jnp.exp(sink - m)
            o_t = jnp.dot(vband, e.astype(BF16), preferred_element_type=F32) * (1.0 / den)
            for pr in range(B_GROUP // 2):
                pair = jnp.concatenate([o_t[:, (2 * pr) * w:(2 * pr + 1) * w],
                                        o_t[:, (2 * pr + 1) * w:(2 * pr + 2) * w]], axis=0)
                col = (B_GROUP * g + 2 * pr) * B_HEAD_DIM
                o_scr[nb * w:(nb + 1) * w, col:col + 2 * B_HEAD_DIM] = pair.T
    og = (o_scr[...] * _silu(zb_ref[0].astype(F32))).astype(BF16)
    out_ref[0] = h1_ref[0] + mod_ref[0, 7:8, :] * jnp.dot(og, wo_ref[...], preferred_element_type=F32)


def _final(h1, qb, ks, vst, zb, wo, mod, bias, sink):
    bsz, seq, d = h1.shape
    nt = seq // TM
    nblk = TM // WINDOW
    tok = lambda: pl.BlockSpec((1, TM, d), lambda b, t: (b, t, 0))
    prev = lambda t: jnp.maximum(t * nblk - 1, 0)
    return pl.pallas_call(
        _final_kernel,
        grid=(bsz, nt),
        in_specs=[tok(),
                  pl.BlockSpec((1, B_HEADS, TM, LANES), lambda b, t: (b, 0, t, 0)),
                  pl.BlockSpec((1, B_KV_HEADS, TM, LANES), lambda b, t: (b, 0, t, 0)),
                  pl.BlockSpec((1, B_KV_HEADS, WINDOW, LANES), lambda b, t: (b, 0, prev(t), 0)),
                  pl.BlockSpec((1, B_KV_HEADS, B_HEAD_DIM, TM), lambda b, t: (b, 0, 0, t)),
                  pl.BlockSpec((1, B_KV_HEADS, B_HEAD_DIM, WINDOW), lambda b, t: (b, 0, 0, prev(t))),
                  tok(),
                  _const_spec(wo.shape),
                  pl.BlockSpec((1, 8, d), lambda b, t: (b, 0, 0)),
                  _const_spec(bias.shape), _const_spec(sink.shape)],
        out_specs=tok(),
        out_shape=jax.ShapeDtypeStruct((bsz, seq, d), F32),
        scratch_shapes=[pltpu.VMEM((TM, d), F32)],
        compiler_params=_params(),
        name="attn_b_out",
    )(h1, qb, ks, ks, vst, vst, zb, wo, mod, bias, sink)


def _pad_heads(w, n_heads, head_dim):
    d = w.shape[0]
    w = w.reshape(d, n_heads, head_dim)
    w = jnp.pad(w, ((0, 0), (0, 0), (0, LANES - head_dim)))
    return w.reshape(d, n_heads * LANES)


def _pad_gain(g, head_dim, scale):
    return jnp.pad(g.astype(F32) * scale, (0, LANES - head_dim)).reshape(1, LANES)


def _swa_bias_table():
    slopes = np.asarray(_alibi_slopes(B_HEADS), dtype=np.float32)
    k = np.arange(2 * WINDOW)[:, None]
    q = np.arange(WINDOW)[None, :]
    dist = q + WINDOW - k
    valid = (dist >= 0) & (dist < WINDOW)
    tab = np.empty((B_KV_HEADS, 2 * WINDOW, B_GROUP * WINDOW), np.float32)
    for g in range(B_KV_HEADS):
        for hh in range(B_GROUP):
            blk = np.where(valid, -slopes[B_GROUP * g + hh] * dist.astype(np.float32), np.float32(NEG))
            tab[g, :, hh * WINDOW:(hh + 1) * WINDOW] = blk
    return jnp.asarray(tab)


def kernel(x, c, a_norm_g, a_ada_w, a_ada_b, a_w_in, a_q_norm_g, a_k_norm_g, a_lambda_q1, a_lambda_k1,
           a_lambda_q2, a_lambda_k2, a_subln_g, a_w_out, kv_norm_g, kv_ada_w, kv_ada_b, w_kv, kv_k_norm_g,
           b_norm_g, b_ada_w, b_ada_b, b_w_in, b_q_norm_g, b_sinks, b_w_out):
    bsz, seq, d = x.shape
    assert a_norm_g.shape[0] == 1 and b_norm_g.shape[0] == 1, "one layer of each mixer"
    assert TM == TK and seq % TM == 0 and seq % TQ == 0 and TQ % TK == 0 and TM % WINDOW == 0
    a_inner = A_HEADS * 2 * A_HEAD_DIM
    b_inner = B_HEADS * B_HEAD_DIM
    n_sub = 2 * A_HEADS

    mod = _modulation(c, a_ada_w[0], a_ada_b[0], kv_ada_w, kv_ada_b, b_ada_w[0], b_ada_b[0])

    w_in = a_w_in[0]
    wq = _pad_heads(w_in[:, 0:a_inner], n_sub, A_HEAD_DIM)
    wk = _pad_heads(w_in[:, a_inner:2 * a_inner], n_sub, A_HEAD_DIM)
    wqk = jnp.concatenate([wq, wk], axis=1).astype(BF16)
    wvt = w_in[:, 2 * a_inner:3 * a_inner].T.astype(BF16)
    wz = w_in[:, 3 * a_inner:].astype(BF16)
    q_scale = A_HEAD_DIM ** -0.5 * LOG2E
    gq = _pad_gain(a_q_norm_g[0], A_HEAD_DIM, q_scale)
    gk = _pad_gain(a_k_norm_g[0], A_HEAD_DIM, 1.0)
    bound = (A_HEAD_DIM * BOUND_SLACK * jnp.max(jnp.abs(gq)) * jnp.max(jnp.abs(gk))).reshape(1)

    qp, kp, vt, z = _proj_a(x, mod, a_norm_g[0].reshape(1, d), wqk, wz, wvt, gq, gk)

    lambda_init = 0.8 - 0.6 * math.exp(-0.3 * 0)
    row = lambda v: v.astype(F32).reshape(1, -1)
    og = _attn_a(bound, qp, kp, vt, z, row(a_lambda_q1[0]), row(a_lambda_k1[0]), row(a_lambda_q2[0]),
                 row(a_lambda_k2[0]), row(a_subln_g[0]), lambda_init)

    kv_w = B_KV_HEADS * B_HEAD_DIM
    wks = _pad_heads(w_kv[:, 0:kv_w], B_KV_HEADS, B_HEAD_DIM).astype(BF16)
    wvst = w_kv[:, kv_w:].T.astype(BF16)
    wqb = _pad_heads(b_w_in[0][:, 0:b_inner], B_HEADS, B_HEAD_DIM).astype(BF16)
    wzb = b_w_in[0][:, b_inner:].astype(BF16)
    gks = _pad_gain(kv_k_norm_g, B_HEAD_DIM, 1.0)
    gqb = _pad_gain(b_q_norm_g[0], B_HEAD_DIM, B_HEAD_DIM ** -0.5)

    h1, qb, ks, vst, zb = _mid(x, og, mod, a_w_out[0].astype(BF16), kv_norm_g.reshape(1, d),
                               b_norm_g[0].reshape(1, d), wks, wvst, wqb, wzb, gks, gqb)

    sink = jnp.repeat(b_sinks[0].astype(F32).reshape(B_KV_HEADS, B_GROUP), WINDOW, axis=1)
    sink = sink.reshape(B_KV_HEADS, 1, B_GROUP * WINDOW)
    return _final(h1, qb, ks, vst, zb, b_w_out[0].astype(BF16), mod, _swa_bias_table(), sink)
```

```python
import functools
import math

import numpy as np
import jax
import jax.numpy as jnp
from jax import lax
from jax.experimental import pallas as pl
from jax.experimental.pallas import tpu as pltpu

EPS = 1e-6
LANES = 128
A_HEADS = 8
A_HEAD_DIM = 64
B_HEADS = 16
B_KV_HEADS = 2
B_GROUP = B_HEADS // B_KV_HEADS
B_HEAD_DIM = 64
WINDOW = 128
NEG = -1e30
V_ROWS = 2 * A_HEAD_DIM + 16

TM = 512
TQ = 1024
TK = 512
VMEM_LIMIT = 56 * 1024 * 1024

LOG2E = 1.4426950408889634
EXP2_ZERO = 152.0
BOUND_MAX = 50.0
BOUND_SLACK = 1.01

F32 = jnp.float32
BF16 = jnp.bfloat16
NT_DIMS = (((1,), (1,)), ((), ()))


def _alibi_slopes(n_heads):
    return [2.0 ** (-8.0 * (h + 1) / n_heads) for h in range(n_heads)]


def _alibi_slopes_log2(n_heads):
    return [s * LOG2E for s in _alibi_slopes(n_heads)]


def _silu(x):
    return x * (1.0 / (1.0 + jnp.exp(-x)))


def _const_spec(shape):
    zeros = (0,) * len(shape)
    return pl.BlockSpec(shape, lambda *_: zeros, pipeline_mode=pl.Buffered(1))


def _params():
    return pltpu.CompilerParams(dimension_semantics=("arbitrary",) * 2, vmem_limit_bytes=VMEM_LIMIT)


def _mod_kernel(c_ref, aw_ref, kw_ref, bw_ref, ab_ref, kb_ref, bb_ref, o_ref):
    j = pl.program_id(0)
    sc = _silu(c_ref[...])

    def emit(w_ref, b_ref):
        o_ref[...] = jnp.dot(sc, w_ref[...], precision=lax.Precision.HIGHEST,
                             preferred_element_type=F32) + b_ref[...]

    @pl.when(j < 3)
    def _():
        emit(aw_ref, ab_ref)

    @pl.when((j >= 3) & (j < 5))
    def _():
        emit(kw_ref, kb_ref)

    @pl.when(j >= 5)
    def _():
        emit(bw_ref, bb_ref)


def _modulation(c, a_w, a_b, kv_w, kv_b, b_w, b_b):
    bsz, d = c.shape
    wspec = lambda f: pl.BlockSpec((d, d), lambda j: (0, f(j)))
    bspec = lambda f: pl.BlockSpec((1, d), lambda j: (0, f(j)))
    fa = lambda j: jnp.minimum(j, 2)
    fk = lambda j: jnp.clip(j - 3, 0, 1)
    fb = lambda j: jnp.clip(j - 5, 0, 2)
    out = pl.pallas_call(
        _mod_kernel,
        grid=(8,),
        in_specs=[pl.BlockSpec((bsz, d), lambda j: (0, 0)),
                  wspec(fa), wspec(fk), wspec(fb), bspec(fa), bspec(fk), bspec(fb)],
        out_specs=pl.BlockSpec((bsz, d), lambda j: (0, j)),
        out_shape=jax.ShapeDtypeStruct((bsz, 8 * d), F32),
        compiler_params=pltpu.CompilerParams(dimension_semantics=("arbitrary",),
                                             vmem_limit_bytes=VMEM_LIMIT),
        name="ada_modulation",
    )(c, a_w, kv_w, b_w, a_b.reshape(1, -1), kv_b.reshape(1, -1), b_b.reshape(1, -1))
    return out.reshape(bsz, 8, d)


def _modulated_norm(x, g, scale, shift):
    ms = jnp.mean(x * x, axis=-1, keepdims=True)
    return (x * lax.rsqrt(ms + EPS) * g) * (1.0 + scale) + shift


def _pair_norm(x, gain2, first_half):
    y = x * x
    ss_a = jnp.sum(jnp.where(first_half, y, 0.0), axis=-1, keepdims=True)
    ss_b = jnp.sum(jnp.where(first_half, 0.0, y), axis=-1, keepdims=True)
    ms = jnp.where(first_half, ss_a, ss_b) * (2.0 / LANES)
    return x * lax.rsqrt(ms + EPS) * gain2


def _split_pair(xn, first_half, fill_a, fill_b):
    return (jnp.where(first_half, xn, fill_a),
            jnp.where(first_half, pltpu.roll(xn, LANES // 2, axis=1), fill_b))


def _proj_a_kernel(x_ref, mod_ref, g_ref, wqk_ref, wz_ref, wvt_ref, gq_ref, gk_ref,
                   qp_ref, kp_ref, vt_ref, z_ref, *, slopes):
    tm = x_ref.shape[1]
    u = _modulated_norm(x_ref[0], g_ref[...], mod_ref[0, 1:2, :], mod_ref[0, 0:1, :]).astype(BF16)

    lane = lax.broadcasted_iota(jnp.int32, (tm, LANES), 1)
    first_half = lane < A_HEAD_DIM
    col_hi = lane == A_HEAD_DIM
    col_lo = lane == A_HEAD_DIM + 1
    pos = lax.broadcasted_iota(jnp.int32, (tm, LANES), 0).astype(F32)
    q_fill = jnp.where(col_hi | col_lo, 1.0, 0.0)
    gq = gq_ref[...]
    gk = gk_ref[...]
    a_inner = A_HEADS * 2 * A_HEAD_DIM
    for hp in range(A_HEADS // 2):
        c0 = hp * 2 * LANES
        qq = jnp.dot(u, wqk_ref[:, c0:c0 + 2 * LANES], preferred_element_type=F32)
        kk = jnp.dot(u, wqk_ref[:, a_inner + c0:a_inner + c0 + 2 * LANES], preferred_element_type=F32)
        for hl in range(2):
            h = 2 * hp + hl
            bias = pos * slopes[h]
            bias_hi = bias.astype(BF16).astype(F32)
            k_fill = jnp.where(col_hi, bias_hi, jnp.where(col_lo, bias - bias_hi, 0.0))
            qn = _pair_norm(qq[:, hl * LANES:(hl + 1) * LANES], gq, first_half)
            kn = _pair_norm(kk[:, hl * LANES:(hl + 1) * LANES], gk, first_half)
            for s, (qs, ks) in enumerate(zip(_split_pair(qn, first_half, q_fill, q_fill),
                                             _split_pair(kn, first_half, k_fill, k_fill))):
                qp_ref[0, 2 * h + s] = qs.astype(BF16)
                kp_ref[0, 2 * h + s] = ks.astype(BF16)

    z_ref[0] = jnp.dot(u, wz_ref[...], preferred_element_type=F32).astype(BF16)

    v_t = lax.dot_general(wvt_ref[...], u, NT_DIMS, preferred_element_type=F32)
    pad_rows = V_ROWS - 2 * A_HEAD_DIM
    ones_blk = jnp.where(lax.broadcasted_iota(jnp.int32, (pad_rows, tm), 0) == 0, 1.0, 0.0).astype(BF16)
    vd = 2 * A_HEAD_DIM
    for h in range(A_HEADS):
        vt_ref[0, h, 0, 0:vd, :] = v_t[h * vd:(h + 1) * vd].astype(BF16)
        vt_ref[0, h, 0, vd:V_ROWS, :] = ones_blk


def _proj_a(x, mod, g, wqk, wz, wvt, gq, gk):
    bsz, seq, d = x.shape
    nt = seq // TM
    n_sub = 2 * A_HEADS
    kern = functools.partial(_proj_a_kernel, slopes=_alibi_slopes_log2(A_HEADS))
    return pl.pallas_call(
        kern,
        grid=(bsz, nt),
        in_specs=[pl.BlockSpec((1, TM, d), lambda b, t: (b, t, 0)),
                  pl.BlockSpec((1, 8, d), lambda b, t: (b, 0, 0)),
                  _const_spec((1, d)),
                  _const_spec(wqk.shape), _const_spec(wz.shape), _const_spec(wvt.shape),
                  _const_spec((1, LANES)), _const_spec((1, LANES))],
        out_specs=[pl.BlockSpec((1, n_sub, TM, LANES), lambda b, t: (b, 0, t, 0)),
                   pl.BlockSpec((1, n_sub, TM, LANES), lambda b, t: (b, 0, t, 0)),
                   pl.BlockSpec((1, A_HEADS, 1, V_ROWS, TM), lambda b, t: (b, 0, t, 0, 0)),
                   pl.BlockSpec((1, TM, d), lambda b, t: (b, t, 0))],
        out_shape=[jax.ShapeDtypeStruct((bsz, n_sub, seq, LANES), BF16),
                   jax.ShapeDtypeStruct((bsz, n_sub, seq, LANES), BF16),
                   jax.ShapeDtypeStruct((bsz, A_HEADS, nt, V_ROWS, TM), BF16),
                   jax.ShapeDtypeStruct((bsz, seq, d), BF16)],
        compiler_params=_params(),
        name="proj_a",
    )(x, mod, g, wqk, wz, wvt, gq, gk)


def _attn_a_kernel(bound_ref, qp_ref, kp_ref, vt_ref, z_ref, lq1_ref, lk1_ref, lq2_ref, lk2_ref, gs_ref,
                   og_ref, acc_ref, m_ref, *, slopes2, lambda_init):
    h = pl.program_id(1)
    i = pl.program_id(2)
    tq = qp_ref.shape[2]
    tk = kp_ref.shape[3]
    vd = 2 * A_HEAD_DIM

    slope2 = jnp.float32(slopes2[0])
    zero_dist = jnp.int32(math.ceil(EXP2_ZERO / slopes2[0]))
    for hh in range(1, A_HEADS):
        slope2 = jnp.where(h == hh, jnp.float32(slopes2[hh]), slope2)
        zero_dist = jnp.where(h == hh, jnp.int32(math.ceil(EXP2_ZERO / slopes2[hh])), zero_dist)

    acc_ref[...] = jnp.zeros(acc_ref.shape, F32)
    bound = bound_ref[0]

    def tile_offset(j):
        return slope2 * (j * tk - i * tq).astype(F32)

    n_diag = tq // tk

    def causal_mask(width):
        krow = lax.broadcasted_iota(jnp.int32, (tk, width), 0)
        qcol = lax.broadcasted_iota(jnp.int32, (tk, width), 1)
        return krow <= qcol

    @pl.when(bound <= BOUND_MAX)
    def _():
        q_local = lax.broadcasted_iota(jnp.int32, (1, tq), 1).astype(F32)
        ref = bound + slope2 * q_local

        def kv_tile(j, diag=None):
            q0 = 0 if diag is None else diag * tk
            off = tile_offset(j) - ref[:, q0:]
            v_t = vt_ref[0, 0, j]
            for sub in range(2):
                s = lax.dot_general(kp_ref[0, sub, j], qp_ref[0, sub, q0:, :], NT_DIMS,
                                    preferred_element_type=F32)
                e = jnp.exp2(s + off)
                if diag is not None:
                    e = jnp.where(causal_mask(tq - q0), e, 0.0)
                acc_ref[sub, :, q0:] += jnp.dot(v_t, e.astype(BF16), preferred_element_type=F32)

        def body(j, carry):
            kv_tile(j)
            return carry

        first = jnp.maximum(i * tq - zero_dist + 1, 0) // tk
        lax.fori_loop(first, i * n_diag, body, 0)
        for r in range(n_diag):
            kv_tile(i * n_diag + r, diag=r)

    @pl.when(bound > BOUND_MAX)
    def _():
        m_ref[...] = jnp.full(m_ref.shape, NEG, F32)

        def kv_tile(j, diag=None):
            q0 = 0 if diag is None else diag * tk
            c = tile_offset(j)
            v_t = vt_ref[0, 0, j]
            for sub in range(2):
                s = lax.dot_general(kp_ref[0, sub, j], qp_ref[0, sub, q0:, :], NT_DIMS,
                                    preferred_element_type=F32)
                if diag is not None:
                    s = jnp.where(causal_mask(tq - q0), s, NEG)
                m_old = m_ref[sub, :, q0:]
                m_new = jnp.maximum(m_old, jnp.max(s, axis=0, keepdims=True) + c)
                alpha = jnp.exp2(m_old - m_new)
                p = jnp.exp2(s - (m_new - c)).astype(BF16)
                pv = jnp.dot(v_t, p, preferred_element_type=F32)
                acc_ref[sub, :, q0:] = acc_ref[sub, :, q0:] * alpha + pv
                m_ref[sub, :, q0:] = m_new

        def body(j, carry):
            kv_tile(j)
            return carry

        lax.fori_loop(0, i * n_diag, body, 0)
        for r in range(n_diag):
            kv_tile(i * n_diag + r, diag=r)

    lam = (jnp.exp(jnp.sum(lq1_ref[...] * lk1_ref[...], axis=-1, keepdims=True))
           - jnp.exp(jnp.sum(lq2_ref[...] * lk2_ref[...], axis=-1, keepdims=True))
           + lambda_init)
    a1 = acc_ref[0]
    a2 = acc_ref[1]
    o_t = (a1[0:vd] * (1.0 / a1[vd:vd + 1]) - lam * (a2[0:vd] * (1.0 / a2[vd:vd + 1])))
    ms = jnp.mean(o_t * o_t, axis=0, keepdims=True)
    o = (o_t * lax.rsqrt(ms + EPS)).T
    o = o * gs_ref[...] * (1.0 - lambda_init)
    og_ref[0] = (o * _silu(z_ref[0].astype(F32))).astype(BF16)


def _attn_a(bound, qp, kp, vt, z, lq1, lk1, lq2, lk2, gs, lambda_init):
    bsz, n_sub, seq, _ = qp.shape
    nk = seq // TK
    nq = seq // TQ
    d = z.shape[-1]
    vd = 2 * A_HEAD_DIM
    kp5 = kp.reshape(bsz, n_sub, nk, TK, LANES)
    kern = functools.partial(_attn_a_kernel, slopes2=_alibi_slopes_log2(A_HEADS), lambda_init=lambda_init)
    vec = lambda n: pl.BlockSpec((1, n), lambda b, h, i: (0, 0))
    return pl.pallas_call(
        kern,
        grid=(bsz, A_HEADS, nq),
        in_specs=[pl.BlockSpec(memory_space=pltpu.SMEM),
                  pl.BlockSpec((1, 2, TQ, LANES), lambda b, h, i: (b, h, i, 0)),
                  pl.BlockSpec((1, 2, nk, TK, LANES), lambda b, h, i: (b, h, 0, 0, 0)),
                  pl.BlockSpec((1, 1, nk, V_ROWS, TK), lambda b, h, i: (b, h, 0, 0, 0)),
                  pl.BlockSpec((1, TQ, vd), lambda b, h, i: (b, i, h)),
                  vec(A_HEAD_DIM), vec(A_HEAD_DIM), vec(A_HEAD_DIM), vec(A_HEAD_DIM), vec(vd)],
        out_specs=pl.BlockSpec((1, TQ, vd), lambda b, h, i: (b, i, h)),
        out_shape=jax.ShapeDtypeStruct((bsz, seq, d), BF16),
        scratch_shapes=[pltpu.VMEM((2, V_ROWS, TQ), F32), pltpu.VMEM((2, 1, TQ), F32)],
        compiler_params=pltpu.CompilerParams(dimension_semantics=("arbitrary",) * 3,
                                             vmem_limit_bytes=VMEM_LIMIT),
        name="attn_a",
    )(bound, qp, kp5, vt, z, lq1, lk1, lq2, lk2, gs)


def _mid_kernel(x_ref, og_ref, mod_ref, wo_ref, gkv_ref, gb_ref, wk_ref, wvt_ref, wqb_ref, wzb_ref,
                gks_ref, gqb_ref, h1_ref, qb_ref, ks_ref, vst_ref, zb_ref):
    gate = mod_ref[0, 2:3, :]
    h1 = x_ref[0] + gate * jnp.dot(og_ref[0], wo_ref[...], preferred_element_type=F32)
    h1_ref[0] = h1

    ms = jnp.mean(h1 * h1, axis=-1, keepdims=True)
    r = h1 * lax.rsqrt(ms + EPS)
    ukv = ((r * gkv_ref[...]) * (1.0 + mod_ref[0, 4:5, :]) + mod_ref[0, 3:4, :]).astype(BF16)
    ub = ((r * gb_ref[...]) * (1.0 + mod_ref[0, 6:7, :]) + mod_ref[0, 5:6, :]).astype(BF16)

    tm = x_ref.shape[1]
    first_half = lax.broadcasted_iota(jnp.int32, (tm, LANES), 1) < B_HEAD_DIM

    kk = jnp.dot(ukv, wk_ref[...], preferred_element_type=F32)
    ks_ref[0] = _pair_norm(kk, gks_ref[...], first_half).astype(BF16)
    v_t = lax.dot_general(wvt_ref[...], ukv, NT_DIMS, preferred_element_type=F32)
    for g in range(B_KV_HEADS):
        vst_ref[0, g] = v_t[g * B_HEAD_DIM:(g + 1) * B_HEAD_DIM].astype(BF16)

    gqb = gqb_ref[...]
    for pp in range(B_HEADS // 4):
        qq = jnp.dot(ub, wqb_ref[:, pp * 2 * LANES:(pp + 1) * 2 * LANES], preferred_element_type=F32)
        for hl in range(2):
            h0 = 4 * pp + 2 * hl
            qn = _pair_norm(qq[:, hl * LANES:(hl + 1) * LANES], gqb, first_half)
            swapped = pltpu.roll(qn, LANES // 2, axis=1)
            if h0 < B_GROUP:
                qb_ref[0, h0] = jnp.where(first_half, qn, 0.0).astype(BF16)
                qb_ref[0, h0 + 1] = jnp.where(first_half, swapped, 0.0).astype(BF16)
            else:
                qb_ref[0, h0] = jnp.where(first_half, 0.0, swapped).astype(BF16)
                qb_ref[0, h0 + 1] = jnp.where(first_half, 0.0, qn).astype(BF16)
    zb_ref[0] = jnp.dot(ub, wzb_ref[...], preferred_element_type=F32).astype(BF16)


def _mid(x, og, mod, wo, gkv, gb, wk, wvt, wqb, wzb, gks, gqb):
    bsz, seq, d = x.shape
    nt = seq // TM
    tok = lambda: pl.BlockSpec((1, TM, d), lambda b, t: (b, t, 0))
    return pl.pallas_call(
        _mid_kernel,
        grid=(bsz, nt),
        in_specs=[tok(), tok(),
                  pl.BlockSpec((1, 8, d), lambda b, t: (b, 0, 0)),
                  _const_spec(wo.shape), _const_spec((1, d)), _const_spec((1, d)),
                  _const_spec(wk.shape), _const_spec(wvt.shape), _const_spec(wqb.shape),
                  _const_spec(wzb.shape), _const_spec((1, LANES)), _const_spec((1, LANES))],
        out_specs=[tok(),
                   pl.BlockSpec((1, B_HEADS, TM, LANES), lambda b, t: (b, 0, t, 0)),
                   pl.BlockSpec((1, TM, LANES), lambda b, t: (b, t, 0)),
                   pl.BlockSpec((1, B_KV_HEADS, B_HEAD_DIM, TM), lambda b, t: (b, 0, 0, t)),
                   tok()],
        out_shape=[jax.ShapeDtypeStruct((bsz, seq, d), F32),
                   jax.ShapeDtypeStruct((bsz, B_HEADS, seq, LANES), BF16),
                   jax.ShapeDtypeStruct((bsz, seq, LANES), BF16),
                   jax.ShapeDtypeStruct((bsz, B_KV_HEADS, B_HEAD_DIM, seq), BF16),
                   jax.ShapeDtypeStruct((bsz, seq, d), BF16)],
        compiler_params=_params(),
        name="mid_proj",
    )(x, og, mod, wo, gkv, gb, wk, wvt, wqb, wzb, gks, gqb)


def _final_kernel(h1_ref, qb_ref, ksc_ref, ksp_ref, vtc_ref, vtp_ref, zb_ref, wo_ref, mod_ref,
                  bias_ref, sink_ref, out_ref, o_scr):
    t = pl.program_id(1)
    tm = h1_ref.shape[1]
    w = WINDOW
    first_tile = t == 0
    prev_rows = lax.broadcasted_iota(jnp.int32, (2 * w, B_GROUP * w), 0) < w
    for nb in range(tm // w):
        if nb == 0:
            kband = jnp.concatenate([ksp_ref[0], ksc_ref[0, 0:w, :]], axis=0)
        else:
            kband = ksc_ref[0, (nb - 1) * w:(nb + 1) * w, :]
        for g in range(B_KV_HEADS):
            bias = bias_ref[g]
            sink = sink_ref[g]
            if nb == 0:
                vband = jnp.concatenate([vtp_ref[0, g], vtc_ref[0, g, :, 0:w]], axis=1)
            else:
                vband = vtc_ref[0, g, :, (nb - 1) * w:(nb + 1) * w]
            qs = jnp.concatenate([qb_ref[0, B_GROUP * g + hh, nb * w:(nb + 1) * w, :]
                                  for hh in range(B_GROUP)], axis=0)
            s = lax.dot_general(kband, qs, NT_DIMS, preferred_element_type=F32) + bias
            if nb == 0:
                s = jnp.where(first_tile & prev_rows, NEG, s)
            m = jnp.maximum(jnp.max(s, axis=0, keepdims=True), sink)
            e = jnp.exp2(s - m)
            den = jnp.sum(e, axis=0, keepdims=True) + jnp.exp2(sink - m)
            o_t = jnp.dot(vband, e.astype(BF16), preferred_element_type=F32) * (1.0 / den)
            for pr in range(B_GROUP // 2):
                pair = jnp.concatenate([o_t[:, (2 * pr) * w:(2 * pr + 1) * w],
                                        o_t[:, (2 * pr + 1) * w:(2 * pr + 2) * w]], axis=0)
                col = (B_GROUP * g + 2 * pr) * B_HEAD_DIM
                o_scr[nb * w:(nb + 1) * w, col:col + 2 * B_HEAD_DIM] = pair.T
    og = (o_scr[...] * _silu(zb_ref[0].astype(F32))).astype(BF16)
    out_ref[0] = h1_ref[0] + mod_ref[0, 7:8, :] * jnp.dot(og, wo_ref[...], preferred_element_type=F32)


def _final(h1, qb, ks, vst, zb, wo, mod, bias, sink):
    bsz, seq, d = h1.shape
    nt = seq // TM
    nblk = TM // WINDOW
    tok = lambda: pl.BlockSpec((1, TM, d), lambda b, t: (b, t, 0))
    prev = lambda t: jnp.maximum(t * nblk - 1, 0)
    return pl.pallas_call(
        _final_kernel,
        grid=(bsz, nt),
        in_specs=[tok(),
                  pl.BlockSpec((1, B_HEADS, TM, LANES), lambda b, t: (b, 0, t, 0)),
                  pl.BlockSpec((1, TM, LANES), lambda b, t: (b, t, 0)),
                  pl.BlockSpec((1, WINDOW, LANES), lambda b, t: (b, prev(t), 0)),
                  pl.BlockSpec((1, B_KV_HEADS, B_HEAD_DIM, TM), lambda b, t: (b, 0, 0, t)),
                  pl.BlockSpec((1, B_KV_HEADS, B_HEAD_DIM, WINDOW), lambda b, t: (b, 0, 0, prev(t))),
                  tok(),
                  _const_spec(wo.shape),
                  pl.BlockSpec((1, 8, d), lambda b, t: (b, 0, 0)),
                  _const_spec(bias.shape), _const_spec(sink.shape)],
        out_specs=tok(),
        out_shape=jax.ShapeDtypeStruct((bsz, seq, d), F32),
        scratch_shapes=[pltpu.VMEM((TM, d), F32)],
        compiler_params=_params(),
        name="attn_b_out",
    )(h1, qb, ks, ks, vst, vst, zb, wo, mod, bias, sink)


def _pair_gain(g, scale):
    return jnp.tile(g.astype(F32) * scale, 2).reshape(1, LANES)


def _swa_bias_table():
    slopes = np.asarray(_alibi_slopes_log2(B_HEADS), dtype=np.float32)
    k = np.arange(2 * WINDOW)[:, None]
    q = np.arange(WINDOW)[None, :]
    dist = q + WINDOW - k
    valid = (dist >= 0) & (dist < WINDOW)
    tab = np.empty((B_KV_HEADS, 2 * WINDOW, B_GROUP * WINDOW), np.float32)
    for g in range(B_KV_HEADS):
        for hh in range(B_GROUP):
            blk = np.where(valid, -slopes[B_GROUP * g + hh] * dist.astype(np.float32), np.float32(NEG))
            tab[g, :, hh * WINDOW:(hh + 1) * WINDOW] = blk
    return jnp.asarray(tab)


def kernel(x, c, a_norm_g, a_ada_w, a_ada_b, a_w_in, a_q_norm_g, a_k_norm_g, a_lambda_q1, a_lambda_k1,
           a_lambda_q2, a_lambda_k2, a_subln_g, a_w_out, kv_norm_g, kv_ada_w, kv_ada_b, w_kv, kv_k_norm_g,
           b_norm_g, b_ada_w, b_ada_b, b_w_in, b_q_norm_g, b_sinks, b_w_out):
    bsz, seq, d = x.shape
    assert a_norm_g.shape[0] == 1 and b_norm_g.shape[0] == 1, "one layer of each mixer"
    assert TM == TK and seq % TM == 0 and seq % TQ == 0 and TQ % TK == 0 and TM % WINDOW == 0
    a_inner = A_HEADS * 2 * A_HEAD_DIM
    b_inner = B_HEADS * B_HEAD_DIM

    mod = _modulation(c, a_ada_w[0], a_ada_b[0], kv_ada_w, kv_ada_b, b_ada_w[0], b_ada_b[0])

    w_in = a_w_in[0]
    wqk = w_in[:, 0:2 * a_inner].astype(BF16)
    wvt = w_in[:, 2 * a_inner:3 * a_inner].T.astype(BF16)
    wz = w_in[:, 3 * a_inner:].astype(BF16)
    gq = _pair_gain(a_q_norm_g[0], A_HEAD_DIM ** -0.5 * LOG2E)
    gk = _pair_gain(a_k_norm_g[0], 1.0)
    bound = (A_HEAD_DIM * BOUND_SLACK * jnp.max(jnp.abs(gq)) * jnp.max(jnp.abs(gk))).reshape(1)

    qp, kp, vt, z = _proj_a(x, mod, a_norm_g[0].reshape(1, d), wqk, wz, wvt, gq, gk)

    lambda_init = 0.8 - 0.6 * math.exp(-0.3 * 0)
    row = lambda v: v.astype(F32).reshape(1, -1)
    og = _attn_a(bound, qp, kp, vt, z, row(a_lambda_q1[0]), row(a_lambda_k1[0]), row(a_lambda_q2[0]),
                 row(a_lambda_k2[0]), row(a_subln_g[0]), lambda_init)

    kv_w = B_KV_HEADS * B_HEAD_DIM
    wks = w_kv[:, 0:kv_w].astype(BF16)
    wvst = w_kv[:, kv_w:].T.astype(BF16)
    wqb = b_w_in[0][:, 0:b_inner].astype(BF16)
    wzb = b_w_in[0][:, b_inner:].astype(BF16)
    gks = _pair_gain(kv_k_norm_g, 1.0)
    gqb = _pair_gain(b_q_norm_g[0], B_HEAD_DIM ** -0.5 * LOG2E)

    h1, qb, ks, vst, zb = _mid(x, og, mod, a_w_out[0].astype(BF16), kv_norm_g.reshape(1, d),
                               b_norm_g[0].reshape(1, d), wks, wvst, wqb, wzb, gks, gqb)

    sink = jnp.repeat(b_sinks[0].astype(F32).reshape(B_KV_HEADS, B_GROUP) * LOG2E, WINDOW, axis=1)
    sink = sink.reshape(B_KV_HEADS, 1, B_GROUP * WINDOW)
    return _final(h1, qb, ks, vst, zb, b_w_out[0].astype(BF16), mod, _swa_bias_table(), sink)
```

```python
import functools
import math

import numpy as np
import jax
import jax.numpy as jnp
from jax import lax
from jax.experimental import pallas as pl
from jax.experimental.pallas import tpu as pltpu

EPS = 1e-6
LANES = 128
A_HEADS = 8
A_HEAD_DIM = 64
B_HEADS = 16
B_KV_HEADS = 2
B_GROUP = B_HEADS // B_KV_HEADS
B_HEAD_DIM = 64
WINDOW = 128
NEG = -1e30
V_ROWS = 2 * A_HEAD_DIM + 16

TM = 512
TQ = 1024
TK = 512
A_BATCH_ROWS = 2
VMEM_LIMIT = 56 * 1024 * 1024

LOG2E = 1.4426950408889634
EXP2_ZERO = 152.0
BOUND_MAX = 50.0
BOUND_SLACK = 1.01

F32 = jnp.float32
BF16 = jnp.bfloat16
NT_DIMS = (((1,), (1,)), ((), ()))


def _alibi_slopes(n_heads):
    return [2.0 ** (-8.0 * (h + 1) / n_heads) for h in range(n_heads)]


def _alibi_slopes_log2(n_heads):
    return [s * LOG2E for s in _alibi_slopes(n_heads)]


def _silu(x):
    return x * (1.0 / (1.0 + jnp.exp(-x)))


def _const_spec(shape):
    zeros = (0,) * len(shape)
    return pl.BlockSpec(shape, lambda *_: zeros, pipeline_mode=pl.Buffered(1))


def _params():
    return pltpu.CompilerParams(dimension_semantics=("arbitrary",) * 2, vmem_limit_bytes=VMEM_LIMIT)


def _mod_kernel(c_ref, aw_ref, kw_ref, bw_ref, ab_ref, kb_ref, bb_ref, o_ref):
    j = pl.program_id(0)
    sc = _silu(c_ref[...])

    def emit(w_ref, b_ref):
        o_ref[...] = jnp.dot(sc, w_ref[...], precision=lax.Precision.HIGHEST,
                             preferred_element_type=F32) + b_ref[...]

    @pl.when(j < 3)
    def _():
        emit(aw_ref, ab_ref)

    @pl.when((j >= 3) & (j < 5))
    def _():
        emit(kw_ref, kb_ref)

    @pl.when(j >= 5)
    def _():
        emit(bw_ref, bb_ref)


def _modulation(c, a_w, a_b, kv_w, kv_b, b_w, b_b):
    bsz, d = c.shape
    wspec = lambda f: pl.BlockSpec((d, d), lambda j: (0, f(j)))
    bspec = lambda f: pl.BlockSpec((1, d), lambda j: (0, f(j)))
    fa = lambda j: jnp.minimum(j, 2)
    fk = lambda j: jnp.clip(j - 3, 0, 1)
    fb = lambda j: jnp.clip(j - 5, 0, 2)
    out = pl.pallas_call(
        _mod_kernel,
        grid=(8,),
        in_specs=[pl.BlockSpec((bsz, d), lambda j: (0, 0)),
                  wspec(fa), wspec(fk), wspec(fb), bspec(fa), bspec(fk), bspec(fb)],
        out_specs=pl.BlockSpec((bsz, d), lambda j: (0, j)),
        out_shape=jax.ShapeDtypeStruct((bsz, 8 * d), F32),
        compiler_params=pltpu.CompilerParams(dimension_semantics=("arbitrary",),
                                             vmem_limit_bytes=VMEM_LIMIT),
        name="ada_modulation",
    )(c, a_w, kv_w, b_w, a_b.reshape(1, -1), kv_b.reshape(1, -1), b_b.reshape(1, -1))
    return out.reshape(bsz, 8, d)


def _modulated_norm(x, g, scale, shift):
    ms = jnp.mean(x * x, axis=-1, keepdims=True)
    return (x * lax.rsqrt(ms + EPS) * g) * (1.0 + scale) + shift


def _pair_norm(x, gain2, first_half):
    y = x * x
    ss_a = jnp.sum(jnp.where(first_half, y, 0.0), axis=-1, keepdims=True)
    ss_b = jnp.sum(jnp.where(first_half, 0.0, y), axis=-1, keepdims=True)
    ms = jnp.where(first_half, ss_a, ss_b) * (2.0 / LANES)
    return x * lax.rsqrt(ms + EPS) * gain2


def _split_pair(xn, first_half, fill_a, fill_b):
    return (jnp.where(first_half, xn, fill_a),
            jnp.where(first_half, pltpu.roll(xn, LANES // 2, axis=1), fill_b))


def _proj_a_kernel(x_ref, mod_ref, g_ref, wqk_ref, wz_ref, wvt_ref, gq_ref, gk_ref,
                   qp_ref, kp_ref, vt_ref, z_ref, *, slopes):
    tm = x_ref.shape[1]
    u = _modulated_norm(x_ref[0], g_ref[...], mod_ref[0, 1:2, :], mod_ref[0, 0:1, :]).astype(BF16)

    lane = lax.broadcasted_iota(jnp.int32, (tm, LANES), 1)
    first_half = lane < A_HEAD_DIM
    col_hi = lane == A_HEAD_DIM
    col_lo = lane == A_HEAD_DIM + 1
    pos = lax.broadcasted_iota(jnp.int32, (tm, LANES), 0).astype(F32)
    q_fill = jnp.where(col_hi | col_lo, 1.0, 0.0)
    gq = gq_ref[...]
    gk = gk_ref[...]
    a_inner = A_HEADS * 2 * A_HEAD_DIM
    for hp in range(A_HEADS // 2):
        c0 = hp * 2 * LANES
        qq = jnp.dot(u, wqk_ref[:, c0:c0 + 2 * LANES], preferred_element_type=F32)
        kk = jnp.dot(u, wqk_ref[:, a_inner + c0:a_inner + c0 + 2 * LANES], preferred_element_type=F32)
        for hl in range(2):
            h = 2 * hp + hl
            bias = pos * slopes[h]
            bias_hi = bias.astype(BF16).astype(F32)
            k_fill = jnp.where(col_hi, bias_hi, jnp.where(col_lo, bias - bias_hi, 0.0))
            qn = _pair_norm(qq[:, hl * LANES:(hl + 1) * LANES], gq, first_half)
            kn = _pair_norm(kk[:, hl * LANES:(hl + 1) * LANES], gk, first_half)
            for s, (qs, ks) in enumerate(zip(_split_pair(qn, first_half, q_fill, q_fill),
                                             _split_pair(kn, first_half, k_fill, k_fill))):
                qp_ref[0, 2 * h + s] = qs.astype(BF16)
                kp_ref[0, 2 * h + s] = ks.astype(BF16)

    z_ref[0] = jnp.dot(u, wz_ref[...], preferred_element_type=F32).astype(BF16)

    v_t = lax.dot_general(wvt_ref[...], u, NT_DIMS, preferred_element_type=F32)
    pad_rows = V_ROWS - 2 * A_HEAD_DIM
    ones_blk = jnp.where(lax.broadcasted_iota(jnp.int32, (pad_rows, tm), 0) == 0, 1.0, 0.0).astype(BF16)
    vd = 2 * A_HEAD_DIM
    for h in range(A_HEADS):
        vt_ref[0, h, 0, 0:vd, :] = v_t[h * vd:(h + 1) * vd].astype(BF16)
        vt_ref[0, h, 0, vd:V_ROWS, :] = ones_blk


def _proj_a(x, mod, g, wqk, wz, wvt, gq, gk):
    bsz, seq, d = x.shape
    nt = seq // TM
    n_sub = 2 * A_HEADS
    kern = functools.partial(_proj_a_kernel, slopes=_alibi_slopes_log2(A_HEADS))
    return pl.pallas_call(
        kern,
        grid=(bsz, nt),
        in_specs=[pl.BlockSpec((1, TM, d), lambda b, t: (b, t, 0)),
                  pl.BlockSpec((1, 8, d), lambda b, t: (b, 0, 0)),
                  _const_spec((1, d)),
                  _const_spec(wqk.shape), _const_spec(wz.shape), _const_spec(wvt.shape),
                  _const_spec((1, LANES)), _const_spec((1, LANES))],
        out_specs=[pl.BlockSpec((1, n_sub, TM, LANES), lambda b, t: (b, 0, t, 0)),
                   pl.BlockSpec((1, n_sub, TM, LANES), lambda b, t: (b, 0, t, 0)),
                   pl.BlockSpec((1, A_HEADS, 1, V_ROWS, TM), lambda b, t: (b, 0, t, 0, 0)),
                   pl.BlockSpec((1, TM, d), lambda b, t: (b, t, 0))],
        out_shape=[jax.ShapeDtypeStruct((bsz, n_sub, seq, LANES), BF16),
                   jax.ShapeDtypeStruct((bsz, n_sub, seq, LANES), BF16),
                   jax.ShapeDtypeStruct((bsz, A_HEADS, nt, V_ROWS, TM), BF16),
                   jax.ShapeDtypeStruct((bsz, seq, d), BF16)],
        compiler_params=_params(),
        name="proj_a",
    )(x, mod, g, wqk, wz, wvt, gq, gk)


def _attn_a_kernel(bound_ref, qp_ref, kp_ref, vt_ref, z_ref, lq1_ref, lk1_ref, lq2_ref, lk2_ref, gs_ref,
                   og_ref, acc_ref, m_ref, *, slopes2, lambda_init):
    h = pl.program_id(1)
    i = pl.program_id(2)
    nb = qp_ref.shape[0]
    tq = qp_ref.shape[2]
    tk = kp_ref.shape[3]
    vd = 2 * A_HEAD_DIM

    slope2 = jnp.float32(slopes2[0])
    zero_dist = jnp.int32(math.ceil(EXP2_ZERO / slopes2[0]))
    for hh in range(1, A_HEADS):
        slope2 = jnp.where(h == hh, jnp.float32(slopes2[hh]), slope2)
        zero_dist = jnp.where(h == hh, jnp.int32(math.ceil(EXP2_ZERO / slopes2[hh])), zero_dist)

    acc_ref[...] = jnp.zeros(acc_ref.shape, F32)
    bound = bound_ref[0]

    def tile_offset(j):
        return slope2 * (j * tk - i * tq).astype(F32)

    n_diag = tq // tk

    def causal_mask(width):
        krow = lax.broadcasted_iota(jnp.int32, (tk, width), 0)
        qcol = lax.broadcasted_iota(jnp.int32, (tk, width), 1)
        return krow <= qcol

    @pl.when(bound <= BOUND_MAX)
    def _():
        q_local = lax.broadcasted_iota(jnp.int32, (1, tq), 1).astype(F32)
        ref = bound + slope2 * q_local

        def kv_tile(j, diag=None):
            q0 = 0 if diag is None else diag * tk
            off = tile_offset(j) - ref[:, q0:]
            for bb in range(nb):
                v_t = vt_ref[bb, 0, j]
                for sub in range(2):
                    s = lax.dot_general(kp_ref[bb, sub, j], qp_ref[bb, sub, q0:, :], NT_DIMS,
                                        preferred_element_type=F32)
                    e = jnp.exp2(s + off)
                    if diag is not None:
                        e = jnp.where(causal_mask(tq - q0), e, 0.0)
                    acc_ref[bb, sub, :, q0:] += jnp.dot(v_t, e.astype(BF16), preferred_element_type=F32)

        def body(j, carry):
            kv_tile(j)
            return carry

        first = jnp.maximum(i * tq - zero_dist + 1, 0) // tk
        lax.fori_loop(first, i * n_diag, body, 0)
        for r in range(n_diag):
            kv_tile(i * n_diag + r, diag=r)

    @pl.when(bound > BOUND_MAX)
    def _():
        m_ref[...] = jnp.full(m_ref.shape, NEG, F32)

        def kv_tile(j, diag=None):
            q0 = 0 if diag is None else diag * tk
            c = tile_offset(j)
            for bb in range(nb):
                v_t = vt_ref[bb, 0, j]
                for sub in range(2):
                    s = lax.dot_general(kp_ref[bb, sub, j], qp_ref[bb, sub, q0:, :], NT_DIMS,
                                        preferred_element_type=F32)
                    if diag is not None:
                        s = jnp.where(causal_mask(tq - q0), s, NEG)
                    m_old = m_ref[bb, sub, :, q0:]
                    m_new = jnp.maximum(m_old, jnp.max(s, axis=0, keepdims=True) + c)
                    alpha = jnp.exp2(m_old - m_new)
                    p = jnp.exp2(s - (m_new - c)).astype(BF16)
                    pv = jnp.dot(v_t, p, preferred_element_type=F32)
                    acc_ref[bb, sub, :, q0:] = acc_ref[bb, sub, :, q0:] * alpha + pv
                    m_ref[bb, sub, :, q0:] = m_new

        def body(j, carry):
            kv_tile(j)
            return carry

        lax.fori_loop(0, i * n_diag, body, 0)
        for r in range(n_diag):
            kv_tile(i * n_diag + r, diag=r)

    lam = (jnp.exp(jnp.sum(lq1_ref[...] * lk1_ref[...], axis=-1, keepdims=True))
           - jnp.exp(jnp.sum(lq2_ref[...] * lk2_ref[...], axis=-1, keepdims=True))
           + lambda_init)
    for bb in range(nb):
        a1 = acc_ref[bb, 0]
        a2 = acc_ref[bb, 1]
        o_t = (a1[0:vd] * (1.0 / a1[vd:vd + 1]) - lam * (a2[0:vd] * (1.0 / a2[vd:vd + 1])))
        ms = jnp.mean(o_t * o_t, axis=0, keepdims=True)
        o = (o_t * lax.rsqrt(ms + EPS)).T
        o = o * gs_ref[...] * (1.0 - lambda_init)
        og_ref[bb] = (o * _silu(z_ref[bb].astype(F32))).astype(BF16)


def _attn_a(bound, qp, kp, vt, z, lq1, lk1, lq2, lk2, gs, lambda_init):
    bsz, n_sub, seq, _ = qp.shape
    nk = seq // TK
    nq = seq // TQ
    d = z.shape[-1]
    vd = 2 * A_HEAD_DIM
    kp5 = kp.reshape(bsz, n_sub, nk, TK, LANES)
    nb = A_BATCH_ROWS if bsz % A_BATCH_ROWS == 0 else 1
    kern = functools.partial(_attn_a_kernel, slopes2=_alibi_slopes_log2(A_HEADS), lambda_init=lambda_init)
    vec = lambda n: pl.BlockSpec((1, n), lambda b, h, i: (0, 0))
    return pl.pallas_call(
        kern,
        grid=(bsz // nb, A_HEADS, nq),
        in_specs=[pl.BlockSpec(memory_space=pltpu.SMEM),
                  pl.BlockSpec((nb, 2, TQ, LANES), lambda b, h, i: (b, h, i, 0)),
                  pl.BlockSpec((nb, 2, nk, TK, LANES), lambda b, h, i: (b, h, 0, 0, 0)),
                  pl.BlockSpec((nb, 1, nk, V_ROWS, TK), lambda b, h, i: (b, h, 0, 0, 0)),
                  pl.BlockSpec((nb, TQ, vd), lambda b, h, i: (b, i, h)),
                  vec(A_HEAD_DIM), vec(A_HEAD_DIM), vec(A_HEAD_DIM), vec(A_HEAD_DIM), vec(vd)],
        out_specs=pl.BlockSpec((nb, TQ, vd), lambda b, h, i: (b, i, h)),
        out_shape=jax.ShapeDtypeStruct((bsz, seq, d), BF16),
        scratch_shapes=[pltpu.VMEM((nb, 2, V_ROWS, TQ), F32), pltpu.VMEM((nb, 2, 1, TQ), F32)],
        compiler_params=pltpu.CompilerParams(dimension_semantics=("arbitrary",) * 3,
                                             vmem_limit_bytes=VMEM_LIMIT),
        name="attn_a",
    )(bound, qp, kp5, vt, z, lq1, lk1, lq2, lk2, gs)


def _mid_kernel(x_ref, og_ref, mod_ref, wo_ref, gkv_ref, gb_ref, wk_ref, wvt_ref, wqb_ref, wzb_ref,
                gks_ref, gqb_ref, h1_ref, qb_ref, ks_ref, vst_ref, zb_ref):
    gate = mod_ref[0, 2:3, :]
    h1 = x_ref[0] + gate * jnp.dot(og_ref[0], wo_ref[...], preferred_element_type=F32)
    h1_ref[0] = h1

    ms = jnp.mean(h1 * h1, axis=-1, keepdims=True)
    r = h1 * lax.rsqrt(ms + EPS)
    ukv = ((r * gkv_ref[...]) * (1.0 + mod_ref[0, 4:5, :]) + mod_ref[0, 3:4, :]).astype(BF16)
    ub = ((r * gb_ref[...]) * (1.0 + mod_ref[0, 6:7, :]) + mod_ref[0, 5:6, :]).astype(BF16)

    tm = x_ref.shape[1]
    first_half = lax.broadcasted_iota(jnp.int32, (tm, LANES), 1) < B_HEAD_DIM

    kk = jnp.dot(ukv, wk_ref[...], preferred_element_type=F32)
    ks_ref[0] = _pair_norm(kk, gks_ref[...], first_half).astype(BF16)
    v_t = lax.dot_general(wvt_ref[...], ukv, NT_DIMS, preferred_element_type=F32)
    for g in range(B_KV_HEADS):
        vst_ref[0, g] = v_t[g * B_HEAD_DIM:(g + 1) * B_HEAD_DIM].astype(BF16)

    gqb = gqb_ref[...]
    for pp in range(B_HEADS // 4):
        qq = jnp.dot(ub, wqb_ref[:, pp * 2 * LANES:(pp + 1) * 2 * LANES], preferred_element_type=F32)
        for hl in range(2):
            h0 = 4 * pp + 2 * hl
            qn = _pair_norm(qq[:, hl * LANES:(hl + 1) * LANES], gqb, first_half)
            swapped = pltpu.roll(qn, LANES // 2, axis=1)
            if h0 < B_GROUP:
                qb_ref[0, h0] = jnp.where(first_half, qn, 0.0).astype(BF16)
                qb_ref[0, h0 + 1] = jnp.where(first_half, swapped, 0.0).astype(BF16)
            else:
                qb_ref[0, h0] = jnp.where(first_half, 0.0, swapped).astype(BF16)
                qb_ref[0, h0 + 1] = jnp.where(first_half, 0.0, qn).astype(BF16)
    zb_ref[0] = jnp.dot(ub, wzb_ref[...], preferred_element_type=F32).astype(BF16)


def _mid(x, og, mod, wo, gkv, gb, wk, wvt, wqb, wzb, gks, gqb):
    bsz, seq, d = x.shape
    nt = seq // TM
    tok = lambda: pl.BlockSpec((1, TM, d), lambda b, t: (b, t, 0))
    return pl.pallas_call(
        _mid_kernel,
        grid=(bsz, nt),
        in_specs=[tok(), tok(),
                  pl.BlockSpec((1, 8, d), lambda b, t: (b, 0, 0)),
                  _const_spec(wo.shape), _const_spec((1, d)), _const_spec((1, d)),
                  _const_spec(wk.shape), _const_spec(wvt.shape), _const_spec(wqb.shape),
                  _const_spec(wzb.shape), _const_spec((1, LANES)), _const_spec((1, LANES))],
        out_specs=[tok(),
                   pl.BlockSpec((1, B_HEADS, TM, LANES), lambda b, t: (b, 0, t, 0)),
                   pl.BlockSpec((1, TM, LANES), lambda b, t: (b, t, 0)),
                   pl.BlockSpec((1, B_KV_HEADS, B_HEAD_DIM, TM), lambda b, t: (b, 0, 0, t)),
                   tok()],
        out_shape=[jax.ShapeDtypeStruct((bsz, seq, d), F32),
                   jax.ShapeDtypeStruct((bsz, B_HEADS, seq, LANES), BF16),
                   jax.ShapeDtypeStruct((bsz, seq, LANES), BF16),
                   jax.ShapeDtypeStruct((bsz, B_KV_HEADS, B_HEAD_DIM, seq), BF16),
                   jax.ShapeDtypeStruct((bsz, seq, d), BF16)],
        compiler_params=_params(),
        name="mid_proj",
    )(x, og, mod, wo, gkv, gb, wk, wvt, wqb, wzb, gks, gqb)


def _final_kernel(bound_ref, h1_ref, qb_ref, ksc_ref, ksp_ref, vtc_ref, vtp_ref, zb_ref, wo_ref, mod_ref,
                  bias_ref, sink_ref, out_ref, o_scr):
    t = pl.program_id(1)
    tm = h1_ref.shape[1]
    w = WINDOW
    first_tile = t == 0
    prev_rows = lax.broadcasted_iota(jnp.int32, (2 * w, B_GROUP * w), 0) < w
    bound = bound_ref[0]

    def attend(bounded):
        if bounded:
            ref = [jnp.maximum(sink_ref[g], bound) for g in range(B_KV_HEADS)]
            bias_rel = [bias_ref[g] - ref[g] for g in range(B_KV_HEADS)]
            sink_term = [jnp.exp2(sink_ref[g] - ref[g]) for g in range(B_KV_HEADS)]
        for nb in range(tm // w):
            if nb == 0:
                kband = jnp.concatenate([ksp_ref[0], ksc_ref[0, 0:w, :]], axis=0)
            else:
                kband = ksc_ref[0, (nb - 1) * w:(nb + 1) * w, :]
            for g in range(B_KV_HEADS):
                if nb == 0:
                    vband = jnp.concatenate([vtp_ref[0, g], vtc_ref[0, g, :, 0:w]], axis=1)
                else:
                    vband = vtc_ref[0, g, :, (nb - 1) * w:(nb + 1) * w]
                qs = jnp.concatenate([qb_ref[0, B_GROUP * g + hh, nb * w:(nb + 1) * w, :]
                                      for hh in range(B_GROUP)], axis=0)
                s = lax.dot_general(kband, qs, NT_DIMS, preferred_element_type=F32)
                before_start = first_tile & prev_rows
                if bounded:
                    e = jnp.exp2(s + bias_rel[g])
                    if nb == 0:
                        e = jnp.where(before_start, 0.0, e)
                    den = jnp.sum(e, axis=0, keepdims=True) + sink_term[g]
                else:
                    s = s + bias_ref[g]
                    if nb == 0:
                        s = jnp.where(before_start, NEG, s)
                    sink = sink_ref[g]
                    m = jnp.maximum(jnp.max(s, axis=0, keepdims=True), sink)
                    e = jnp.exp2(s - m)
                    den = jnp.sum(e, axis=0, keepdims=True) + jnp.exp2(sink - m)
                o_t = jnp.dot(vband, e.astype(BF16), preferred_element_type=F32) * (1.0 / den)
                for pr in range(B_GROUP // 2):
                    pair = jnp.concatenate([o_t[:, (2 * pr) * w:(2 * pr + 1) * w],
                                            o_t[:, (2 * pr + 1) * w:(2 * pr + 2) * w]], axis=0)
                    col = (B_GROUP * g + 2 * pr) * B_HEAD_DIM
                    o_scr[nb * w:(nb + 1) * w, col:col + 2 * B_HEAD_DIM] = pair.T

    @pl.when(bound <= BOUND_MAX)
    def _():
        attend(True)

    @pl.when(bound > BOUND_MAX)
    def _():
        attend(False)

    og =(o_scr[...] * _silu(zb_ref[0].astype(F32))).astype(BF16)
    out_ref[0] = h1_ref[0] + mod_ref[0, 7:8, :] * jnp.dot(og, wo_ref[...], preferred_element_type=F32)


def _final(bound, h1, qb, ks, vst, zb, wo, mod, bias, sink):
    bsz, seq, d = h1.shape
    nt = seq // TM
    nblk = TM // WINDOW
    tok = lambda: pl.BlockSpec((1, TM, d), lambda b, t: (b, t, 0))
    prev = lambda t: jnp.maximum(t * nblk - 1, 0)
    return pl.pallas_call(
        _final_kernel,
        grid=(bsz, nt),
        in_specs=[pl.BlockSpec(memory_space=pltpu.SMEM),
                  tok(),
                  pl.BlockSpec((1, B_HEADS, TM, LANES), lambda b, t: (b, 0, t, 0)),
                  pl.BlockSpec((1, TM, LANES), lambda b, t: (b, t, 0)),
                  pl.BlockSpec((1, WINDOW, LANES), lambda b, t: (b, prev(t), 0)),
                  pl.BlockSpec((1, B_KV_HEADS, B_HEAD_DIM, TM), lambda b, t: (b, 0, 0, t)),
                  pl.BlockSpec((1, B_KV_HEADS, B_HEAD_DIM, WINDOW), lambda b, t: (b, 0, 0, prev(t))),
                  tok(),
                  _const_spec(wo.shape),
                  pl.BlockSpec((1, 8, d), lambda b, t: (b, 0, 0)),
                  _const_spec(bias.shape), _const_spec(sink.shape)],
        out_specs=tok(),
        out_shape=jax.ShapeDtypeStruct((bsz, seq, d), F32),
        scratch_shapes=[pltpu.VMEM((TM, d), F32)],
        compiler_params=_params(),
        name="attn_b_out",
    )(bound, h1, qb, ks, ks, vst, vst, zb, wo, mod, bias, sink)


def _logit_bound(gq, gk, head_dim):
    return (head_dim * BOUND_SLACK * jnp.max(jnp.abs(gq)) * jnp.max(jnp.abs(gk))).reshape(1)


def _pair_gain(g, scale):
    return jnp.tile(g.astype(F32) * scale, 2).reshape(1, LANES)


def _swa_bias_table():
    slopes = np.asarray(_alibi_slopes_log2(B_HEADS), dtype=np.float32)
    k = np.arange(2 * WINDOW)[:, None]
    q = np.arange(WINDOW)[None, :]
    dist = q + WINDOW - k
    valid = (dist >= 0) & (dist < WINDOW)
    tab = np.empty((B_KV_HEADS, 2 * WINDOW, B_GROUP * WINDOW), np.float32)
    for g in range(B_KV_HEADS):
        for hh in range(B_GROUP):
            blk = np.where(valid, -slopes[B_GROUP * g + hh] * dist.astype(np.float32), np.float32(NEG))
            tab[g, :, hh * WINDOW:(hh + 1) * WINDOW] = blk
    return jnp.asarray(tab)


def kernel(x, c, a_norm_g, a_ada_w, a_ada_b, a_w_in, a_q_norm_g, a_k_norm_g, a_lambda_q1, a_lambda_k1,
           a_lambda_q2, a_lambda_k2, a_subln_g, a_w_out, kv_norm_g, kv_ada_w, kv_ada_b, w_kv, kv_k_norm_g,
           b_norm_g, b_ada_w, b_ada_b, b_w_in, b_q_norm_g, b_sinks, b_w_out):
    bsz, seq, d = x.shape
    assert a_norm_g.shape[0] == 1 and b_norm_g.shape[0] == 1, "one layer of each mixer"
    assert TM == TK and seq % TM == 0 and seq % TQ == 0 and TQ % TK == 0 and TM % WINDOW == 0
    a_inner = A_HEADS * 2 * A_HEAD_DIM
    b_inner = B_HEADS * B_HEAD_DIM

    mod = _modulation(c, a_ada_w[0], a_ada_b[0], kv_ada_w, kv_ada_b, b_ada_w[0], b_ada_b[0])

    w_in = a_w_in[0]
    wqk = w_in[:, 0:2 * a_inner].astype(BF16)
    wvt = w_in[:, 2 * a_inner:3 * a_inner].T.astype(BF16)
    wz = w_in[:, 3 * a_inner:].astype(BF16)
    gq = _pair_gain(a_q_norm_g[0], A_HEAD_DIM ** -0.5 * LOG2E)
    gk = _pair_gain(a_k_norm_g[0], 1.0)
    bound_a = _logit_bound(gq, gk, A_HEAD_DIM)

    qp, kp, vt, z = _proj_a(x, mod, a_norm_g[0].reshape(1, d), wqk, wz, wvt, gq, gk)

    lambda_init = 0.8 - 0.6 * math.exp(-0.3 * 0)
    row = lambda v: v.astype(F32).reshape(1, -1)
    og = _attn_a(bound_a, qp, kp, vt, z, row(a_lambda_q1[0]), row(a_lambda_k1[0]), row(a_lambda_q2[0]),
                 row(a_lambda_k2[0]), row(a_subln_g[0]), lambda_init)

    kv_w = B_KV_HEADS * B_HEAD_DIM
    wks = w_kv[:, 0:kv_w].astype(BF16)
    wvst = w_kv[:, kv_w:].T.astype(BF16)
    wqb = b_w_in[0][:, 0:b_inner].astype(BF16)
    wzb = b_w_in[0][:, b_inner:].astype(BF16)
    gks = _pair_gain(kv_k_norm_g, 1.0)
    gqb = _pair_gain(b_q_norm_g[0], B_HEAD_DIM ** -0.5 * LOG2E)

    h1, qb, ks, vst, zb = _mid(x, og, mod, a_w_out[0].astype(BF16), kv_norm_g.reshape(1, d),
                               b_norm_g[0].reshape(1, d), wks, wvst, wqb, wzb, gks, gqb)

    sink = jnp.repeat(b_sinks[0].astype(F32).reshape(B_KV_HEADS, B_GROUP) * LOG2E, WINDOW, axis=1)
    sink = sink.reshape(B_KV_HEADS, 1, B_GROUP * WINDOW)
    bound_b = _logit_bound(gqb, gks, B_HEAD_DIM)
    return _final(bound_b, h1, qb, ks, vst, zb, b_w_out[0].astype(BF16), mod, _swa_bias_table(), sink)
```

```python
import functools
import math

import numpy as np
import jax
import jax.numpy as jnp
from jax import lax
from jax.experimental import pallas as pl
from jax.experimental.pallas import tpu as pltpu

EPS = 1e-6
LANES = 128
A_HEADS = 8
A_HEAD_DIM = 64
B_HEADS = 16
B_KV_HEADS = 2
B_GROUP = B_HEADS // B_KV_HEADS
B_HEAD_DIM = 64
WINDOW = 128
NEG = -1e30
V_ROWS = 2 * A_HEAD_DIM

TM = 512
TQ = 1024
TK = 512
A_BATCH_ROWS = 2
VMEM_LIMIT = 56 * 1024 * 1024

LOG2E = 1.4426950408889634
EXP2_ZERO = 152.0
BOUND_MAX = 50.0
BOUND_SLACK = 1.01

F32 = jnp.float32
BF16 = jnp.bfloat16
NT_DIMS = (((1,), (1,)), ((), ()))


def _alibi_slopes(n_heads):
    return [2.0 ** (-8.0 * (h + 1) / n_heads) for h in range(n_heads)]


def _alibi_slopes_log2(n_heads):
    return [s * LOG2E for s in _alibi_slopes(n_heads)]


def _silu(x):
    return x * (1.0 / (1.0 + jnp.exp(-x)))


def _const_spec(shape):
    zeros = (0,) * len(shape)
    return pl.BlockSpec(shape, lambda *_: zeros, pipeline_mode=pl.Buffered(1))


def _params():
    return pltpu.CompilerParams(dimension_semantics=("arbitrary",) * 2, vmem_limit_bytes=VMEM_LIMIT)


def _mod_kernel(c_ref, aw_ref, kw_ref, bw_ref, ab_ref, kb_ref, bb_ref, o_ref):
    j = pl.program_id(0)
    sc = _silu(c_ref[...])

    def emit(w_ref, b_ref):
        o_ref[...] = jnp.dot(sc, w_ref[...], precision=lax.Precision.HIGHEST,
                             preferred_element_type=F32) + b_ref[...]

    @pl.when(j < 3)
    def _():
        emit(aw_ref, ab_ref)

    @pl.when((j >= 3) & (j < 5))
    def _():
        emit(kw_ref, kb_ref)

    @pl.when(j >= 5)
    def _():
        emit(bw_ref, bb_ref)


def _modulation(c, a_w, a_b, kv_w, kv_b, b_w, b_b):
    bsz, d = c.shape
    wspec = lambda f: pl.BlockSpec((d, d), lambda j: (0, f(j)))
    bspec = lambda f: pl.BlockSpec((1, d), lambda j: (0, f(j)))
    fa = lambda j: jnp.minimum(j, 2)
    fk = lambda j: jnp.clip(j - 3, 0, 1)
    fb = lambda j: jnp.clip(j - 5, 0, 2)
    out = pl.pallas_call(
        _mod_kernel,
        grid=(8,),
        in_specs=[pl.BlockSpec((bsz, d), lambda j: (0, 0)),
                  wspec(fa), wspec(fk), wspec(fb), bspec(fa), bspec(fk), bspec(fb)],
        out_specs=pl.BlockSpec((bsz, d), lambda j: (0, j)),
        out_shape=jax.ShapeDtypeStruct((bsz, 8 * d), F32),
        compiler_params=pltpu.CompilerParams(dimension_semantics=("arbitrary",),
                                             vmem_limit_bytes=VMEM_LIMIT),
        name="ada_modulation",
    )(c, a_w, kv_w, b_w, a_b.reshape(1, -1), kv_b.reshape(1, -1), b_b.reshape(1, -1))
    return out.reshape(bsz, 8, d)


def _modulated_norm(x, g, scale, shift):
    ms = jnp.mean(x * x, axis=-1, keepdims=True)
    return (x * lax.rsqrt(ms + EPS) * g) * (1.0 + scale) + shift


def _pair_norm(x, gain2, first_half):
    y = x * x
    ss_a = jnp.sum(jnp.where(first_half, y, 0.0), axis=-1, keepdims=True)
    ss_b = jnp.sum(jnp.where(first_half, 0.0, y), axis=-1, keepdims=True)
    ms = jnp.where(first_half, ss_a, ss_b) * (2.0 / LANES)
    return x * lax.rsqrt(ms + EPS) * gain2


def _split_pair(xn, first_half, fill_a, fill_b):
    return (jnp.where(first_half, xn, fill_a),
            jnp.where(first_half, pltpu.roll(xn, LANES // 2, axis=1), fill_b))


def _proj_a_kernel(x_ref, mod_ref, g_ref, wqk_ref, wz_ref, wvt_ref, gq_ref, gk_ref,
                   qp_ref, kp_ref, vt_ref, z_ref, *, slopes):
    tm = x_ref.shape[1]
    u = _modulated_norm(x_ref[0], g_ref[...], mod_ref[0, 1:2, :], mod_ref[0, 0:1, :]).astype(BF16)

    lane = lax.broadcasted_iota(jnp.int32, (tm, LANES), 1)
    first_half = lane < A_HEAD_DIM
    col_hi = lane == A_HEAD_DIM
    col_lo = lane == A_HEAD_DIM + 1
    pos = lax.broadcasted_iota(jnp.int32, (tm, LANES), 0).astype(F32)
    q_fill = jnp.where(col_hi | col_lo, 1.0, 0.0)
    gq = gq_ref[...]
    gk = gk_ref[...]
    a_inner = A_HEADS * 2 * A_HEAD_DIM
    for hp in range(A_HEADS // 2):
        c0 = hp * 2 * LANES
        qq = jnp.dot(u, wqk_ref[:, c0:c0 + 2 * LANES], preferred_element_type=F32)
        kk = jnp.dot(u, wqk_ref[:, a_inner + c0:a_inner + c0 + 2 * LANES], preferred_element_type=F32)
        for hl in range(2):
            h = 2 * hp + hl
            bias = pos * slopes[h]
            bias_hi = bias.astype(BF16).astype(F32)
            k_fill = jnp.where(col_hi, bias_hi, jnp.where(col_lo, bias - bias_hi, 0.0))
            qn = _pair_norm(qq[:, hl * LANES:(hl + 1) * LANES], gq, first_half)
            kn = _pair_norm(kk[:, hl * LANES:(hl + 1) * LANES], gk, first_half)
            for s, (qs, ks) in enumerate(zip(_split_pair(qn, first_half, q_fill, q_fill),
                                             _split_pair(kn, first_half, k_fill, k_fill))):
                qp_ref[0, 2 * h + s] = qs.astype(BF16)
                kp_ref[0, 2 * h + s] = ks.astype(BF16)

    z_ref[0] = jnp.dot(u, wz_ref[...], preferred_element_type=F32).astype(BF16)

    v_t = lax.dot_general(wvt_ref[...], u, NT_DIMS, preferred_element_type=F32)
    for h in range(A_HEADS):
        vt_ref[0, h, 0] = v_t[h * V_ROWS:(h + 1) * V_ROWS].astype(BF16)


def _proj_a(x, mod, g, wqk, wz, wvt, gq, gk):
    bsz, seq, d = x.shape
    nt = seq // TM
    n_sub = 2 * A_HEADS
    kern = functools.partial(_proj_a_kernel, slopes=_alibi_slopes_log2(A_HEADS))
    return pl.pallas_call(
        kern,
        grid=(bsz, nt),
        in_specs=[pl.BlockSpec((1, TM, d), lambda b, t: (b, t, 0)),
                  pl.BlockSpec((1, 8, d), lambda b, t: (b, 0, 0)),
                  _const_spec((1, d)),
                  _const_spec(wqk.shape), _const_spec(wz.shape), _const_spec(wvt.shape),
                  _const_spec((1, LANES)), _const_spec((1, LANES))],
        out_specs=[pl.BlockSpec((1, n_sub, TM, LANES), lambda b, t: (b, 0, t, 0)),
                   pl.BlockSpec((1, n_sub, TM, LANES), lambda b, t: (b, 0, t, 0)),
                   pl.BlockSpec((1, A_HEADS, 1, V_ROWS, TM), lambda b, t: (b, 0, t, 0, 0)),
                   pl.BlockSpec((1, TM, d), lambda b, t: (b, t, 0))],
        out_shape=[jax.ShapeDtypeStruct((bsz, n_sub, seq, LANES), BF16),
                   jax.ShapeDtypeStruct((bsz, n_sub, seq, LANES), BF16),
                   jax.ShapeDtypeStruct((bsz, A_HEADS, nt, V_ROWS, TM), BF16),
                   jax.ShapeDtypeStruct((bsz, seq, d), BF16)],
        compiler_params=_params(),
        name="proj_a",
    )(x, mod, g, wqk, wz, wvt, gq, gk)


def _attn_a_kernel(bound_ref, qp_ref, kp_ref, vt_ref, z_ref, lq1_ref, lk1_ref, lq2_ref, lk2_ref, gs_ref,
                   og_ref, acc_ref, l_ref, m_ref, p_ref, *, slopes2, lambda_init):
    h = pl.program_id(1)
    i = pl.program_id(2)
    nb = qp_ref.shape[0]
    tq = qp_ref.shape[2]
    tk = kp_ref.shape[3]
    vd = 2 * A_HEAD_DIM

    slope2 = jnp.float32(slopes2[0])
    zero_dist = jnp.int32(math.ceil(EXP2_ZERO / slopes2[0]))
    for hh in range(1, A_HEADS):
        slope2 = jnp.where(h == hh, jnp.float32(slopes2[hh]), slope2)
        zero_dist = jnp.where(h == hh, jnp.int32(math.ceil(EXP2_ZERO / slopes2[hh])), zero_dist)

    bound = bound_ref[0]

    def tile_offset(j):
        return slope2 * (j * tk - i * tq).astype(F32)

    n_diag = tq // tk

    def causal_mask(width):
        krow = lax.broadcasted_iota(jnp.int32, (tk, width), 0)
        qcol = lax.broadcasted_iota(jnp.int32, (tk, width), 1)
        return krow <= qcol

    @pl.when(bound <= BOUND_MAX)
    def _():
        q_local = lax.broadcasted_iota(jnp.int32, (1, tq), 1).astype(F32)
        ref = bound + slope2 * q_local

        acc_ref[...] = jnp.zeros(acc_ref.shape, F32)
        l_ref[...] = jnp.zeros(l_ref.shape, F32)

        def probs(j, bb, sub, q0):
            s = lax.dot_general(kp_ref[bb, sub, j], qp_ref[bb, sub, q0:, :], NT_DIMS,
                                preferred_element_type=F32)
            return jnp.exp2(s + (tile_offset(j) - ref[:, q0:]))

        def stage_a(j):
            for bb in range(nb):
                for sub in range(2):
                    e = probs(j, bb, sub, 0)
                    l_ref[bb, sub] += jnp.sum(e, axis=0, keepdims=True)
                    p_ref[bb, sub] = e.astype(BF16)

        def stage_b(j):
            for bb in range(nb):
                v_t = vt_ref[bb, 0, j]
                for sub in range(2):
                    acc_ref[bb, sub] += jnp.dot(v_t, p_ref[bb, sub], preferred_element_type=F32)

        def diag_tile(r):
            q0 = r * tk
            j = i * n_diag + r
            for bb in range(nb):
                v_t = vt_ref[bb, 0, j]
                for sub in range(2):
                    e = jnp.where(causal_mask(tq - q0), probs(j, bb, sub, q0), 0.0)
                    l_ref[bb, sub, :, q0:] += jnp.sum(e, axis=0, keepdims=True)
                    acc_ref[bb, sub, :, q0:] += jnp.dot(v_t, e.astype(BF16), preferred_element_type=F32)

        first = jnp.maximum(i * tq - zero_dist + 1, 0) // tk
        last = i * n_diag

        @pl.when(first < last)
        def _():
            stage_a(first)

        @pl.when(first >= last)
        def _():
            p_ref[...] = jnp.zeros(p_ref.shape, BF16)

        def body(j, carry):
            stage_b(j - 1)
            stage_a(j)
            return carry

        lax.fori_loop(first + 1, last, body, 0)
        stage_b(jnp.maximum(last - 1, 0))
        for r in range(n_diag):
            diag_tile(r)

    @pl.when(bound > BOUND_MAX)
    def _():
        acc_ref[...] = jnp.zeros(acc_ref.shape, F32)
        l_ref[...] = jnp.zeros(l_ref.shape, F32)
        m_ref[...] = jnp.full(m_ref.shape, NEG, F32)

        def kv_tile(j, diag=None):
            q0 = 0 if diag is None else diag * tk
            c = tile_offset(j)
            for bb in range(nb):
                v_t = vt_ref[bb, 0, j]
                for sub in range(2):
                    s = lax.dot_general(kp_ref[bb, sub, j], qp_ref[bb, sub, q0:, :], NT_DIMS,
                                        preferred_element_type=F32)
                    if diag is not None:
                        s = jnp.where(causal_mask(tq - q0), s, NEG)
                    m_old = m_ref[bb, sub, :, q0:]
                    m_new = jnp.maximum(m_old, jnp.max(s, axis=0, keepdims=True) + c)
                    alpha = jnp.exp2(m_old - m_new)
                    p = jnp.exp2(s - (m_new - c))
                    pv = jnp.dot(v_t, p.astype(BF16), preferred_element_type=F32)
                    acc_ref[bb, sub, :, q0:] = acc_ref[bb, sub, :, q0:] * alpha + pv
                    l_ref[bb, sub, :, q0:] = (l_ref[bb, sub, :, q0:] * alpha
                                              + jnp.sum(p, axis=0, keepdims=True))
                    m_ref[bb, sub, :, q0:] = m_new

        def body(j, carry):
            kv_tile(j)
            return carry

        lax.fori_loop(0, i * n_diag, body, 0)
        for r in range(n_diag):
            kv_tile(i * n_diag + r, diag=r)

    lam = (jnp.exp(jnp.sum(lq1_ref[...] * lk1_ref[...], axis=-1, keepdims=True))
           - jnp.exp(jnp.sum(lq2_ref[...] * lk2_ref[...], axis=-1, keepdims=True))
           + lambda_init)
    for bb in range(nb):
        o_t = (acc_ref[bb, 0] * (1.0 / l_ref[bb, 0])
               - lam * (acc_ref[bb, 1] * (1.0 / l_ref[bb, 1])))
        ms = jnp.mean(o_t * o_t, axis=0, keepdims=True)
        o = (o_t * lax.rsqrt(ms + EPS)).T
        o = o * gs_ref[...] * (1.0 - lambda_init)
        og_ref[bb] = (o * _silu(z_ref[bb].astype(F32))).astype(BF16)


def _attn_a(bound, qp, kp, vt, z, lq1, lk1, lq2, lk2, gs, lambda_init):
    bsz, n_sub, seq, _ = qp.shape
    nk = seq // TK
    nq = seq // TQ
    d = z.shape[-1]
    vd = 2 * A_HEAD_DIM
    kp5 = kp.reshape(bsz, n_sub, nk, TK, LANES)
    nb = A_BATCH_ROWS if bsz % A_BATCH_ROWS == 0 else 1
    kern = functools.partial(_attn_a_kernel, slopes2=_alibi_slopes_log2(A_HEADS), lambda_init=lambda_init)
    vec = lambda n: pl.BlockSpec((1, n), lambda b, h, i: (0, 0))
    return pl.pallas_call(
        kern,
        grid=(bsz // nb, A_HEADS, nq),
        in_specs=[pl.BlockSpec(memory_space=pltpu.SMEM),
                  pl.BlockSpec((nb, 2, TQ, LANES), lambda b, h, i: (b, h, i, 0)),
                  pl.BlockSpec((nb, 2, nk, TK, LANES), lambda b, h, i: (b, h, 0, 0, 0)),
                  pl.BlockSpec((nb, 1, nk, V_ROWS, TK), lambda b, h, i: (b, h, 0, 0, 0)),
                  pl.BlockSpec((nb, TQ, vd), lambda b, h, i: (b, i, h)),
                  vec(A_HEAD_DIM), vec(A_HEAD_DIM), vec(A_HEAD_DIM), vec(A_HEAD_DIM), vec(vd)],
        out_specs=pl.BlockSpec((nb, TQ, vd), lambda b, h, i: (b, i, h)),
        out_shape=jax.ShapeDtypeStruct((bsz, seq, d), BF16),
        scratch_shapes=[pltpu.VMEM((nb, 2, V_ROWS, TQ), F32), pltpu.VMEM((nb, 2, 1, TQ), F32),
                        pltpu.VMEM((nb, 2, 1, TQ), F32), pltpu.VMEM((nb, 2, TK, TQ), BF16)],
        compiler_params=pltpu.CompilerParams(dimension_semantics=("arbitrary",) * 3,
                                             vmem_limit_bytes=VMEM_LIMIT),
        name="attn_a",
    )(bound, qp, kp5, vt, z, lq1, lk1, lq2, lk2, gs)


def _mid_kernel(x_ref, og_ref, mod_ref, wo_ref, gkv_ref, gb_ref, wk_ref, wvt_ref, wqb_ref, wzb_ref,
                gks_ref, gqb_ref, h1_ref, qb_ref, ks_ref, vst_ref, zb_ref):
    gate = mod_ref[0, 2:3, :]
    h1 = x_ref[0] + gate * jnp.dot(og_ref[0], wo_ref[...], preferred_element_type=F32)
    h1_ref[0] = h1

    ms = jnp.mean(h1 * h1, axis=-1, keepdims=True)
    r = h1 * lax.rsqrt(ms + EPS)
    ukv = ((r * gkv_ref[...]) * (1.0 + mod_ref[0, 4:5, :]) + mod_ref[0, 3:4, :]).astype(BF16)
    ub = ((r * gb_ref[...]) * (1.0 + mod_ref[0, 6:7, :]) + mod_ref[0, 5:6, :]).astype(BF16)

    tm = x_ref.shape[1]
    first_half = lax.broadcasted_iota(jnp.int32, (tm, LANES), 1) < B_HEAD_DIM

    kk = jnp.dot(ukv, wk_ref[...], preferred_element_type=F32)
    ks_ref[0] = _pair_norm(kk, gks_ref[...], first_half).astype(BF16)
    v_t = lax.dot_general(wvt_ref[...], ukv, NT_DIMS, preferred_element_type=F32)
    for g in range(B_KV_HEADS):
        vst_ref[0, g] = v_t[g * B_HEAD_DIM:(g + 1) * B_HEAD_DIM].astype(BF16)

    gqb = gqb_ref[...]
    for pp in range(B_HEADS // 4):
        qq = jnp.dot(ub, wqb_ref[:, pp * 2 * LANES:(pp + 1) * 2 * LANES], preferred_element_type=F32)
        for hl in range(2):
            h0 = 4 * pp + 2 * hl
            qn = _pair_norm(qq[:, hl * LANES:(hl + 1) * LANES], gqb, first_half)
            swapped = pltpu.roll(qn, LANES // 2, axis=1)
            if h0 < B_GROUP:
                qb_ref[0, h0] = jnp.where(first_half, qn, 0.0).astype(BF16)
                qb_ref[0, h0 + 1] = jnp.where(first_half, swapped, 0.0).astype(BF16)
            else:
                qb_ref[0, h0] = jnp.where(first_half, 0.0, swapped).astype(BF16)
                qb_ref[0, h0 + 1] = jnp.where(first_half, 0.0, qn).astype(BF16)
    zb_ref[0] = jnp.dot(ub, wzb_ref[...], preferred_element_type=F32).astype(BF16)


def _mid(x, og, mod, wo, gkv, gb, wk, wvt, wqb, wzb, gks, gqb):
    bsz, seq, d = x.shape
    nt = seq // TM
    tok = lambda: pl.BlockSpec((1, TM, d), lambda b, t: (b, t, 0))
    return pl.pallas_call(
        _mid_kernel,
        grid=(bsz, nt),
        in_specs=[tok(), tok(),
                  pl.BlockSpec((1, 8, d), lambda b, t: (b, 0, 0)),
                  _const_spec(wo.shape), _const_spec((1, d)), _const_spec((1, d)),
                  _const_spec(wk.shape), _const_spec(wvt.shape), _const_spec(wqb.shape),
                  _const_spec(wzb.shape), _const_spec((1, LANES)), _const_spec((1, LANES))],
        out_specs=[tok(),
                   pl.BlockSpec((1, B_HEADS, TM, LANES), lambda b, t: (b, 0, t, 0)),
                   pl.BlockSpec((1, TM, LANES), lambda b, t: (b, t, 0)),
                   pl.BlockSpec((1, B_KV_HEADS, B_HEAD_DIM, TM), lambda b, t: (b, 0, 0, t)),
                   tok()],
        out_shape=[jax.ShapeDtypeStruct((bsz, seq, d), F32),
                   jax.ShapeDtypeStruct((bsz, B_HEADS, seq, LANES), BF16),
                   jax.ShapeDtypeStruct((bsz, seq, LANES), BF16),
                   jax.ShapeDtypeStruct((bsz, B_KV_HEADS, B_HEAD_DIM, seq), BF16),
                   jax.ShapeDtypeStruct((bsz, seq, d), BF16)],
        compiler_params=_params(),
        name="mid_proj",
    )(x, og, mod, wo, gkv, gb, wk, wvt, wqb, wzb, gks, gqb)


def _final_kernel(bound_ref, h1_ref, qb_ref, ksc_ref, ksp_ref, vtc_ref, vtp_ref, zb_ref, wo_ref, mod_ref,
                  bias_ref, sink_ref, out_ref, o_scr):
    t = pl.program_id(1)
    tm = h1_ref.shape[1]
    w = WINDOW
    first_tile = t == 0
    prev_rows = lax.broadcasted_iota(jnp.int32, (2 * w, B_GROUP * w), 0) < w
    bound = bound_ref[0]

    def attend(bounded):
        if bounded:
            ref = [jnp.maximum(sink_ref[g], bound) for g in range(B_KV_HEADS)]
            bias_rel = [bias_ref[g] - ref[g] for g in range(B_KV_HEADS)]
            sink_term = [jnp.exp2(sink_ref[g] - ref[g]) for g in range(B_KV_HEADS)]
        for nb in range(tm // w):
            if nb == 0:
                kband = jnp.concatenate([ksp_ref[0], ksc_ref[0, 0:w, :]], axis=0)
            else:
                kband = ksc_ref[0, (nb - 1) * w:(nb + 1) * w, :]
            for g in range(B_KV_HEADS):
                if nb == 0:
                    vband = jnp.concatenate([vtp_ref[0, g], vtc_ref[0, g, :, 0:w]], axis=1)
                else:
                    vband = vtc_ref[0, g, :, (nb - 1) * w:(nb + 1) * w]
                qs = jnp.concatenate([qb_ref[0, B_GROUP * g + hh, nb * w:(nb + 1) * w, :]
                                      for hh in range(B_GROUP)], axis=0)
                s = lax.dot_general(kband, qs, NT_DIMS, preferred_element_type=F32)
                before_start = first_tile & prev_rows
                if bounded:
                    e = jnp.exp2(s + bias_rel[g])
                    if nb == 0:
                        e = jnp.where(before_start, 0.0, e)
                    den = jnp.sum(e, axis=0, keepdims=True) + sink_term[g]
                else:
                    s = s + bias_ref[g]
                    if nb == 0:
                        s = jnp.where(before_start, NEG, s)
                    sink = sink_ref[g]
                    m = jnp.maximum(jnp.max(s, axis=0, keepdims=True), sink)
                    e = jnp.exp2(s - m)
                    den = jnp.sum(e, axis=0, keepdims=True) + jnp.exp2(sink - m)
                o_t = jnp.dot(vband, e.astype(BF16), preferred_element_type=F32) * (1.0 / den)
                for pr in range(B_GROUP // 2):
                    pair = jnp.concatenate([o_t[:, (2 * pr) * w:(2 * pr + 1) * w],
                                            o_t[:, (2 * pr + 1) * w:(2 * pr + 2) * w]], axis=0)
                    col = (B_GROUP * g + 2 * pr) * B_HEAD_DIM
                    o_scr[nb * w:(nb + 1) * w, col:col + 2 * B_HEAD_DIM] = pair.T

        og = (o_scr[...] * _silu(zb_ref[0].astype(F32))).astype(BF16)
        out_ref[0] = h1_ref[0] + mod_ref[0, 7:8, :] * jnp.dot(og, wo_ref[...], preferred_element_type=F32)

    @pl.when(bound <= BOUND_MAX)
    def _():
        attend(True)

    @pl.when(bound > BOUND_MAX)
    def _():
        attend(False)


def _final(bound, h1, qb, ks, vst, zb, wo, mod, bias, sink):
    bsz, seq, d = h1.shape
    nt = seq // TM
    nblk = TM // WINDOW
    tok = lambda: pl.BlockSpec((1, TM, d), lambda b, t: (b, t, 0))
    prev = lambda t: jnp.maximum(t * nblk - 1, 0)
    return pl.pallas_call(
        _final_kernel,
        grid=(bsz, nt),
        in_specs=[pl.BlockSpec(memory_space=pltpu.SMEM),
                  tok(),
                  pl.BlockSpec((1, B_HEADS, TM, LANES), lambda b, t: (b, 0, t, 0)),
                  pl.BlockSpec((1, TM, LANES), lambda b, t: (b, t, 0)),
                  pl.BlockSpec((1, WINDOW, LANES), lambda b, t: (b, prev(t), 0)),
                  pl.BlockSpec((1, B_KV_HEADS, B_HEAD_DIM, TM), lambda b, t: (b, 0, 0, t)),
                  pl.BlockSpec((1, B_KV_HEADS, B_HEAD_DIM, WINDOW), lambda b, t: (b, 0, 0, prev(t))),
                  tok(),
                  _const_spec(wo.shape),
                  pl.BlockSpec((1, 8, d), lambda b, t: (b, 0, 0)),
                  _const_spec(bias.shape), _const_spec(sink.shape)],
        out_specs=tok(),
        out_shape=jax.ShapeDtypeStruct((bsz, seq, d), F32),
        scratch_shapes=[pltpu.VMEM((TM, d), F32)],
        compiler_params=_params(),
        name="attn_b_out",
    )(bound, h1, qb, ks, ks, vst, vst, zb, wo, mod, bias, sink)


def _logit_bound(gq, gk, head_dim):
    return (head_dim * BOUND_SLACK * jnp.max(jnp.abs(gq)) * jnp.max(jnp.abs(gk))).reshape(1)


def _pair_gain(g, scale):
    return jnp.tile(g.astype(F32) * scale, 2).reshape(1, LANES)


def _swa_bias_table():
    slopes = np.asarray(_alibi_slopes_log2(B_HEADS), dtype=np.float32)
    k = np.arange(2 * WINDOW)[:, None]
    q = np.arange(WINDOW)[None, :]
    dist = q + WINDOW - k
    valid = (dist >= 0) & (dist < WINDOW)
    tab = np.empty((B_KV_HEADS, 2 * WINDOW, B_GROUP * WINDOW), np.float32)
    for g in range(B_KV_HEADS):
        for hh in range(B_GROUP):
            blk = np.where(valid, -slopes[B_GROUP * g + hh] * dist.astype(np.float32), np.float32(NEG))
            tab[g, :, hh * WINDOW:(hh + 1) * WINDOW] = blk
    return jnp.asarray(tab)


def kernel(x, c, a_norm_g, a_ada_w, a_ada_b, a_w_in, a_q_norm_g, a_k_norm_g, a_lambda_q1, a_lambda_k1,
           a_lambda_q2, a_lambda_k2, a_subln_g, a_w_out, kv_norm_g, kv_ada_w, kv_ada_b, w_kv, kv_k_norm_g,
           b_norm_g, b_ada_w, b_ada_b, b_w_in, b_q_norm_g, b_sinks, b_w_out):
    bsz, seq, d = x.shape
    assert a_norm_g.shape[0] == 1 and b_norm_g.shape[0] == 1, "one layer of each mixer"
    assert TM == TK and seq % TM == 0 and seq % TQ == 0 and TQ % TK == 0 and TM % WINDOW == 0
    a_inner = A_HEADS * 2 * A_HEAD_DIM
    b_inner = B_HEADS * B_HEAD_DIM

    mod = _modulation(c, a_ada_w[0], a_ada_b[0], kv_ada_w, kv_ada_b, b_ada_w[0], b_ada_b[0])

    w_in = a_w_in[0]
    wqk = w_in[:, 0:2 * a_inner].astype(BF16)
    wvt = w_in[:, 2 * a_inner:3 * a_inner].T.astype(BF16)
    wz = w_in[:, 3 * a_inner:].astype(BF16)
    gq = _pair_gain(a_q_norm_g[0], A_HEAD_DIM ** -0.5 * LOG2E)
    gk = _pair_gain(a_k_norm_g[0], 1.0)
    bound_a = _logit_bound(gq, gk, A_HEAD_DIM)

    qp, kp, vt, z = _proj_a(x, mod, a_norm_g[0].reshape(1, d), wqk, wz, wvt, gq, gk)

    lambda_init = 0.8 - 0.6 * math.exp(-0.3 * 0)
    row = lambda v: v.astype(F32).reshape(1, -1)
    og = _attn_a(bound_a, qp, kp, vt, z, row(a_lambda_q1[0]), row(a_lambda_k1[0]), row(a_lambda_q2[0]),
                 row(a_lambda_k2[0]), row(a_subln_g[0]), lambda_init)

    kv_w = B_KV_HEADS * B_HEAD_DIM
    wks = w_kv[:, 0:kv_w].astype(BF16)
    wvst = w_kv[:, kv_w:].T.astype(BF16)
    wqb = b_w_in[0][:, 0:b_inner].astype(BF16)
    wzb = b_w_in[0][:, b_inner:].astype(BF16)
    gks = _pair_gain(kv_k_norm_g, 1.0)
    gqb = _pair_gain(b_q_norm_g[0], B_HEAD_DIM ** -0.5 * LOG2E)

    h1, qb, ks, vst, zb = _mid(x, og, mod, a_w_out[0].astype(BF16), kv_norm_g.reshape(1, d),
                               b_norm_g[0].reshape(1, d), wks, wvst, wqb, wzb, gks, gqb)

    sink = jnp.repeat(b_sinks[0].astype(F32).reshape(B_KV_HEADS, B_GROUP) * LOG2E, WINDOW, axis=1)
    sink = sink.reshape(B_KV_HEADS, 1, B_GROUP * WINDOW)
    bound_b = _logit_bound(gqb, gks, B_HEAD_DIM)
    return _final(bound_b, h1, qb, ks, vst, zb, b_w_out[0].astype(BF16), mod, _swa_bias_table(), sink)
```

```python
import functools
import math

import numpy as np
import jax
import jax.numpy as jnp
from jax import lax
from jax.experimental import pallas as pl
from jax.experimental.pallas import tpu as pltpu

EPS = 1e-6
LANES = 128
A_HEADS = 8
A_HEAD_DIM = 64
B_HEADS = 16
B_KV_HEADS = 2
B_GROUP = B_HEADS // B_KV_HEADS
B_HEAD_DIM = 64
WINDOW = 128
NEG = -1e30
V_ROWS = 2 * A_HEAD_DIM

TM = 512
TQ = 1024
TK = 512
A_BATCH_ROWS = 2
VMEM_LIMIT = 56 * 1024 * 1024

LOG2E = 1.4426950408889634
EXP2_ZERO = 152.0
BOUND_MAX = 50.0
BOUND_SLACK = 1.01

F32 = jnp.float32
BF16 = jnp.bfloat16
NT_DIMS = (((1,), (1,)), ((), ()))


def _alibi_slopes(n_heads):
    return [2.0 ** (-8.0 * (h + 1) / n_heads) for h in range(n_heads)]


def _alibi_slopes_log2(n_heads):
    return [s * LOG2E for s in _alibi_slopes(n_heads)]


def _silu(x):
    return x * (1.0 / (1.0 + jnp.exp(-x)))


def _const_spec(shape):
    zeros = (0,) * len(shape)
    return pl.BlockSpec(shape, lambda *_: zeros, pipeline_mode=pl.Buffered(1))


def _rows_per_step(bsz):
    return A_BATCH_ROWS if bsz % A_BATCH_ROWS == 0 else 1


def _params():
    return pltpu.CompilerParams(dimension_semantics=("arbitrary",) * 2, vmem_limit_bytes=VMEM_LIMIT)


def _mod_kernel(c_ref, aw_ref, kw_ref, bw_ref, ab_ref, kb_ref, bb_ref, o_ref):
    j = pl.program_id(0)
    sc = _silu(c_ref[...])

    def emit(w_ref, b_ref):
        o_ref[...] = jnp.dot(sc, w_ref[...], precision=lax.Precision.HIGHEST,
                             preferred_element_type=F32) + b_ref[...]

    @pl.when(j < 3)
    def _():
        emit(aw_ref, ab_ref)

    @pl.when((j >= 3) & (j < 5))
    def _():
        emit(kw_ref, kb_ref)

    @pl.when(j >= 5)
    def _():
        emit(bw_ref, bb_ref)


def _modulation(c, a_w, a_b, kv_w, kv_b, b_w, b_b):
    bsz, d = c.shape
    wspec = lambda f: pl.BlockSpec((d, d), lambda j: (0, f(j)))
    bspec = lambda f: pl.BlockSpec((1, d), lambda j: (0, f(j)))
    fa = lambda j: jnp.minimum(j, 2)
    fk = lambda j: jnp.clip(j - 3, 0, 1)
    fb = lambda j: jnp.clip(j - 5, 0, 2)
    out = pl.pallas_call(
        _mod_kernel,
        grid=(8,),
        in_specs=[pl.BlockSpec((bsz, d), lambda j: (0, 0)),
                  wspec(fa), wspec(fk), wspec(fb), bspec(fa), bspec(fk), bspec(fb)],
        out_specs=pl.BlockSpec((bsz, d), lambda j: (0, j)),
        out_shape=jax.ShapeDtypeStruct((bsz, 8 * d), F32),
        compiler_params=pltpu.CompilerParams(dimension_semantics=("arbitrary",),
                                             vmem_limit_bytes=VMEM_LIMIT),
        name="ada_modulation",
    )(c, a_w, kv_w, b_w, a_b.reshape(1, -1), kv_b.reshape(1, -1), b_b.reshape(1, -1))
    return out.reshape(bsz, 8, d)


def _modulated_norm(x, g, scale, shift):
    ms = jnp.mean(x * x, axis=-1, keepdims=True)
    return (x * lax.rsqrt(ms + EPS) * g) * (1.0 + scale) + shift


def _pair_norm(x, gain2, first_half):
    y = x * x
    ss_a = jnp.sum(jnp.where(first_half, y, 0.0), axis=-1, keepdims=True)
    ss_b = jnp.sum(jnp.where(first_half, 0.0, y), axis=-1, keepdims=True)
    ms = jnp.where(first_half, ss_a, ss_b) * (2.0 / LANES)
    return x * lax.rsqrt(ms + EPS) * gain2


def _split_pair(xn, first_half, fill_a, fill_b):
    return (jnp.where(first_half, xn, fill_a),
            jnp.where(first_half, pltpu.roll(xn, LANES // 2, axis=1), fill_b))


def _proj_a_kernel(x_ref, mod_ref, g_ref, wqk_ref, wz_ref, wvt_ref, gq_ref, gk_ref,
                   qp_ref, kp_ref, vt_ref, z_ref, *, slopes):
    tm = x_ref.shape[1]
    lane = lax.broadcasted_iota(jnp.int32, (tm, LANES), 1)
    first_half = lane < A_HEAD_DIM
    col_hi = lane == A_HEAD_DIM
    col_lo = lane == A_HEAD_DIM + 1
    pos = lax.broadcasted_iota(jnp.int32, (tm, LANES), 0).astype(F32)
    q_fill = jnp.where(col_hi | col_lo, 1.0, 0.0)
    gq = gq_ref[...]
    gk = gk_ref[...]
    a_inner = A_HEADS * 2 * A_HEAD_DIM
    for bb in range(x_ref.shape[0]):
        u = _modulated_norm(x_ref[bb], g_ref[...], mod_ref[bb, 1:2, :], mod_ref[bb, 0:1, :]).astype(BF16)
        for hp in range(A_HEADS // 2):
            c0 = hp * 2 * LANES
            qq = jnp.dot(u, wqk_ref[:, c0:c0 + 2 * LANES], preferred_element_type=F32)
            kk = jnp.dot(u, wqk_ref[:, a_inner + c0:a_inner + c0 + 2 * LANES],
                         preferred_element_type=F32)
            for hl in range(2):
                h = 2 * hp + hl
                bias = pos * slopes[h]
                bias_hi = bias.astype(BF16).astype(F32)
                k_fill = jnp.where(col_hi, bias_hi, jnp.where(col_lo, bias - bias_hi, 0.0))
                qn = _pair_norm(qq[:, hl * LANES:(hl + 1) * LANES], gq, first_half)
                kn = _pair_norm(kk[:, hl * LANES:(hl + 1) * LANES], gk, first_half)
                for s, (qs, ks) in enumerate(zip(_split_pair(qn, first_half, q_fill, q_fill),
                                                 _split_pair(kn, first_half, k_fill, k_fill))):
                    qp_ref[bb, 2 * h + s] = qs.astype(BF16)
                    kp_ref[bb, 2 * h + s] = ks.astype(BF16)

        z_ref[bb] = jnp.dot(u, wz_ref[...], preferred_element_type=F32).astype(BF16)

        v_t = lax.dot_general(wvt_ref[...], u, NT_DIMS, preferred_element_type=F32)
        for h in range(A_HEADS):
            vt_ref[bb, h, 0] = v_t[h * V_ROWS:(h + 1) * V_ROWS].astype(BF16)


def _proj_a(x, mod, g, wqk, wz, wvt, gq, gk):
    bsz, seq, d = x.shape
    nt = seq // TM
    n_sub = 2 * A_HEADS
    kern = functools.partial(_proj_a_kernel, slopes=_alibi_slopes_log2(A_HEADS))
    nr = _rows_per_step(bsz)
    return pl.pallas_call(
        kern,
        grid=(bsz // nr, nt),
        in_specs=[pl.BlockSpec((nr, TM, d), lambda b, t: (b, t, 0)),
                  pl.BlockSpec((nr, 8, d), lambda b, t: (b, 0, 0)),
                  _const_spec((1, d)),
                  _const_spec(wqk.shape), _const_spec(wz.shape), _const_spec(wvt.shape),
                  _const_spec((1, LANES)), _const_spec((1, LANES))],
        out_specs=[pl.BlockSpec((nr, n_sub, TM, LANES), lambda b, t: (b, 0, t, 0)),
                   pl.BlockSpec((nr, n_sub, TM, LANES), lambda b, t: (b, 0, t, 0)),
                   pl.BlockSpec((nr, A_HEADS, 1, V_ROWS, TM), lambda b, t: (b, 0, t, 0, 0)),
                   pl.BlockSpec((nr, TM, d), lambda b, t: (b, t, 0))],
        out_shape=[jax.ShapeDtypeStruct((bsz, n_sub, seq, LANES), BF16),
                   jax.ShapeDtypeStruct((bsz, n_sub, seq, LANES), BF16),
                   jax.ShapeDtypeStruct((bsz, A_HEADS, nt, V_ROWS, TM), BF16),
                   jax.ShapeDtypeStruct((bsz, seq, d), BF16)],
        compiler_params=_params(),
        name="proj_a",
    )(x, mod, g, wqk, wz, wvt, gq, gk)


def _attn_a_kernel(bound_ref, qp_ref, kp_ref, vt_ref, z_ref, lq1_ref, lk1_ref, lq2_ref, lk2_ref, gs_ref,
                   og_ref, acc_ref, l_ref, m_ref, p_ref, *, slopes2, lambda_init):
    h = pl.program_id(1)
    i = pl.program_id(2)
    nb = qp_ref.shape[0]
    tq = qp_ref.shape[2]
    tk = kp_ref.shape[3]
    vd = 2 * A_HEAD_DIM

    slope2 = jnp.float32(slopes2[0])
    zero_dist = jnp.int32(math.ceil(EXP2_ZERO / slopes2[0]))
    for hh in range(1, A_HEADS):
        slope2 = jnp.where(h == hh, jnp.float32(slopes2[hh]), slope2)
        zero_dist = jnp.where(h == hh, jnp.int32(math.ceil(EXP2_ZERO / slopes2[hh])), zero_dist)

    bound = bound_ref[0]

    def tile_offset(j):
        return slope2 * (j * tk - i * tq).astype(F32)

    n_diag = tq // tk

    def causal_mask(width):
        krow = lax.broadcasted_iota(jnp.int32, (tk, width), 0)
        qcol = lax.broadcasted_iota(jnp.int32, (tk, width), 1)
        return krow <= qcol

    @pl.when(bound <= BOUND_MAX)
    def _():
        q_local = lax.broadcasted_iota(jnp.int32, (1, tq), 1).astype(F32)
        ref = bound + slope2 * q_local

        acc_ref[...] = jnp.zeros(acc_ref.shape, F32)
        l_ref[...] = jnp.zeros(l_ref.shape, F32)

        def probs(j, bb, sub, q0):
            s = lax.dot_general(kp_ref[bb, sub, j], qp_ref[bb, sub, q0:, :], NT_DIMS,
                                preferred_element_type=F32)
            return jnp.exp2(s + (tile_offset(j) - ref[:, q0:]))

        def stage_a(j):
            for bb in range(nb):
                for sub in range(2):
                    e = probs(j, bb, sub, 0)
                    l_ref[bb, sub] += jnp.sum(e, axis=0, keepdims=True)
                    p_ref[bb, sub] = e.astype(BF16)

        def stage_b(j):
            for bb in range(nb):
                v_t = vt_ref[bb, 0, j]
                for sub in range(2):
                    acc_ref[bb, sub] += jnp.dot(v_t, p_ref[bb, sub], preferred_element_type=F32)

        def diag_tile(r):
            q0 = r * tk
            j = i * n_diag + r
            for bb in range(nb):
                v_t = vt_ref[bb, 0, j]
                for sub in range(2):
                    e = jnp.where(causal_mask(tq - q0), probs(j, bb, sub, q0), 0.0)
                    l_ref[bb, sub, :, q0:] += jnp.sum(e, axis=0, keepdims=True)
                    acc_ref[bb, sub, :, q0:] += jnp.dot(v_t, e.astype(BF16), preferred_element_type=F32)

        first = jnp.maximum(i * tq - zero_dist + 1, 0) // tk
        last = i * n_diag

        @pl.when(first < last)
        def _():
            stage_a(first)

        @pl.when(first >= last)
        def _():
            p_ref[...] = jnp.zeros(p_ref.shape, BF16)

        def body(j, carry):
            stage_b(j - 1)
            stage_a(j)
            return carry

        lax.fori_loop(first + 1, last, body, 0)
        stage_b(jnp.maximum(last - 1, 0))
        for r in range(n_diag):
            diag_tile(r)

    @pl.when(bound > BOUND_MAX)
    def _():
        acc_ref[...] = jnp.zeros(acc_ref.shape, F32)
        l_ref[...] = jnp.zeros(l_ref.shape, F32)
        m_ref[...] = jnp.full(m_ref.shape, NEG, F32)

        def kv_tile(j, diag=None):
            q0 = 0 if diag is None else diag * tk
            c = tile_offset(j)
            for bb in range(nb):
                v_t = vt_ref[bb, 0, j]
                for sub in range(2):
                    s = lax.dot_general(kp_ref[bb, sub, j], qp_ref[bb, sub, q0:, :], NT_DIMS,
                                        preferred_element_type=F32)
                    if diag is not None:
                        s = jnp.where(causal_mask(tq - q0), s, NEG)
                    m_old = m_ref[bb, sub, :, q0:]
                    m_new = jnp.maximum(m_old, jnp.max(s, axis=0, keepdims=True) + c)
                    alpha = jnp.exp2(m_old - m_new)
                    p = jnp.exp2(s - (m_new - c))
                    pv = jnp.dot(v_t, p.astype(BF16), preferred_element_type=F32)
                    acc_ref[bb, sub, :, q0:] = acc_ref[bb, sub, :, q0:] * alpha + pv
                    l_ref[bb, sub, :, q0:] = (l_ref[bb, sub, :, q0:] * alpha
                                              + jnp.sum(p, axis=0, keepdims=True))
                    m_ref[bb, sub, :, q0:] = m_new

        def body(j, carry):
            kv_tile(j)
            return carry

        lax.fori_loop(0, i * n_diag, body, 0)
        for r in range(n_diag):
            kv_tile(i * n_diag + r, diag=r)

    lam = (jnp.exp(jnp.sum(lq1_ref[...] * lk1_ref[...], axis=-1, keepdims=True))
           - jnp.exp(jnp.sum(lq2_ref[...] * lk2_ref[...], axis=-1, keepdims=True))
           + lambda_init)
    for bb in range(nb):
        o_t = (acc_ref[bb, 0] * (1.0 / l_ref[bb, 0])
               - lam * (acc_ref[bb, 1] * (1.0 / l_ref[bb, 1])))
        ms = jnp.mean(o_t * o_t, axis=0, keepdims=True)
        o = (o_t * lax.rsqrt(ms + EPS)).T
        o = o * gs_ref[...] * (1.0 - lambda_init)
        og_ref[bb] = (o * _silu(z_ref[bb].astype(F32))).astype(BF16)


def _attn_a(bound, qp, kp, vt, z, lq1, lk1, lq2, lk2, gs, lambda_init):
    bsz, n_sub, seq, _ = qp.shape
    nk = seq // TK
    nq = seq // TQ
    d = z.shape[-1]
    vd = 2 * A_HEAD_DIM
    kp5 = kp.reshape(bsz, n_sub, nk, TK, LANES)
    nb = A_BATCH_ROWS if bsz % A_BATCH_ROWS == 0 else 1
    kern = functools.partial(_attn_a_kernel, slopes2=_alibi_slopes_log2(A_HEADS), lambda_init=lambda_init)
    vec = lambda n: pl.BlockSpec((1, n), lambda b, h, i: (0, 0))
    return pl.pallas_call(
        kern,
        grid=(bsz // nb, A_HEADS, nq),
        in_specs=[pl.BlockSpec(memory_space=pltpu.SMEM),
                  pl.BlockSpec((nb, 2, TQ, LANES), lambda b, h, i: (b, h, i, 0)),
                  pl.BlockSpec((nb, 2, nk, TK, LANES), lambda b, h, i: (b, h, 0, 0, 0)),
                  pl.BlockSpec((nb, 1, nk, V_ROWS, TK), lambda b, h, i: (b, h, 0, 0, 0)),
                  pl.BlockSpec((nb, TQ, vd), lambda b, h, i: (b, i, h)),
                  vec(A_HEAD_DIM), vec(A_HEAD_DIM), vec(A_HEAD_DIM), vec(A_HEAD_DIM), vec(vd)],
        out_specs=pl.BlockSpec((nb, TQ, vd), lambda b, h, i: (b, i, h)),
        out_shape=jax.ShapeDtypeStruct((bsz, seq, d), BF16),
        scratch_shapes=[pltpu.VMEM((nb, 2, V_ROWS, TQ), F32), pltpu.VMEM((nb, 2, 1, TQ), F32),
                        pltpu.VMEM((nb, 2, 1, TQ), F32), pltpu.VMEM((nb, 2, TK, TQ), BF16)],
        compiler_params=pltpu.CompilerParams(dimension_semantics=("arbitrary",) * 3,
                                             vmem_limit_bytes=VMEM_LIMIT),
        name="attn_a",
    )(bound, qp, kp5, vt, z, lq1, lk1, lq2, lk2, gs)


def _mid_kernel(x_ref, og_ref, mod_ref, wo_ref, gkv_ref, gb_ref, wk_ref, wvt_ref, wqb_ref, wzb_ref,
                gks_ref, gqb_ref, h1_ref, qb_ref, ks_ref, vst_ref, zb_ref):
    tm = x_ref.shape[1]
    first_half = lax.broadcasted_iota(jnp.int32, (tm, LANES), 1) < B_HEAD_DIM
    gqb = gqb_ref[...]
    for bb in range(x_ref.shape[0]):
        gate = mod_ref[bb, 2:3, :]
        h1 = x_ref[bb] + gate * jnp.dot(og_ref[bb], wo_ref[...], preferred_element_type=F32)
        h1_ref[bb] = h1

        ms = jnp.mean(h1 * h1, axis=-1, keepdims=True)
        r = h1 * lax.rsqrt(ms + EPS)
        ukv = ((r * gkv_ref[...]) * (1.0 + mod_ref[bb, 4:5, :]) + mod_ref[bb, 3:4, :]).astype(BF16)
        ub = ((r * gb_ref[...]) * (1.0 + mod_ref[bb, 6:7, :]) + mod_ref[bb, 5:6, :]).astype(BF16)

        kk = jnp.dot(ukv, wk_ref[...], preferred_element_type=F32)
        ks_ref[bb] = _pair_norm(kk, gks_ref[...], first_half).astype(BF16)
        v_t = lax.dot_general(wvt_ref[...], ukv, NT_DIMS, preferred_element_type=F32)
        for g in range(B_KV_HEADS):
            vst_ref[bb, g] = v_t[g * B_HEAD_DIM:(g + 1) * B_HEAD_DIM].astype(BF16)

        for pp in range(B_HEADS // 4):
            qq = jnp.dot(ub, wqb_ref[:, pp * 2 * LANES:(pp + 1) * 2 * LANES], preferred_element_type=F32)
            for hl in range(2):
                h0 = 4 * pp + 2 * hl
                qn = _pair_norm(qq[:, hl * LANES:(hl + 1) * LANES], gqb, first_half)
                swapped = pltpu.roll(qn, LANES // 2, axis=1)
                if h0 < B_GROUP:
                    qb_ref[bb, h0] = jnp.where(first_half, qn, 0.0).astype(BF16)
                    qb_ref[bb, h0 + 1] = jnp.where(first_half, swapped, 0.0).astype(BF16)
                else:
                    qb_ref[bb, h0] = jnp.where(first_half, 0.0, swapped).astype(BF16)
                    qb_ref[bb, h0 + 1] = jnp.where(first_half, 0.0, qn).astype(BF16)
        zb_ref[bb] = jnp.dot(ub, wzb_ref[...], preferred_element_type=F32).astype(BF16)


def _mid(x, og, mod, wo, gkv, gb, wk, wvt, wqb, wzb, gks, gqb):
    bsz, seq, d = x.shape
    nt = seq // TM
    nr = _rows_per_step(bsz)
    tok = lambda: pl.BlockSpec((nr, TM, d), lambda b, t: (b, t, 0))
    return pl.pallas_call(
        _mid_kernel,
        grid=(bsz // nr, nt),
        in_specs=[tok(), tok(),
                  pl.BlockSpec((nr, 8, d), lambda b, t: (b, 0, 0)),
                  _const_spec(wo.shape), _const_spec((1, d)), _const_spec((1, d)),
                  _const_spec(wk.shape), _const_spec(wvt.shape), _const_spec(wqb.shape),
                  _const_spec(wzb.shape), _const_spec((1, LANES)), _const_spec((1, LANES))],
        out_specs=[tok(),
                   pl.BlockSpec((nr, B_HEADS, TM, LANES), lambda b, t: (b, 0, t, 0)),
                   pl.BlockSpec((nr, TM, LANES), lambda b, t: (b, t, 0)),
                   pl.BlockSpec((nr, B_KV_HEADS, B_HEAD_DIM, TM), lambda b, t: (b, 0, 0, t)),
                   tok()],
        out_shape=[jax.ShapeDtypeStruct((bsz, seq, d), F32),
                   jax.ShapeDtypeStruct((bsz, B_HEADS, seq, LANES), BF16),
                   jax.ShapeDtypeStruct((bsz, seq, LANES), BF16),
                   jax.ShapeDtypeStruct((bsz, B_KV_HEADS, B_HEAD_DIM, seq), BF16),
                   jax.ShapeDtypeStruct((bsz, seq, d), BF16)],
        compiler_params=_params(),
        name="mid_proj",
    )(x, og, mod, wo, gkv, gb, wk, wvt, wqb, wzb, gks, gqb)


def _final_kernel(bound_ref, h1_ref, qb_ref, ksc_ref, ksp_ref, vtc_ref, vtp_ref, zb_ref, wo_ref, mod_ref,
                  bias_ref, sink_ref, out_ref, o_scr):
    t = pl.program_id(1)
    tm = h1_ref.shape[1]
    w = WINDOW
    first_tile = t == 0
    prev_rows = lax.broadcasted_iota(jnp.int32, (2 * w, B_GROUP * w), 0) < w
    bound = bound_ref[0]

    def attend(bounded):
        if bounded:
            ref = [jnp.maximum(sink_ref[g], bound) for g in range(B_KV_HEADS)]
            ref_terms = ([bias_ref[g] - ref[g] for g in range(B_KV_HEADS)],
                         [jnp.exp2(sink_ref[g] - ref[g]) for g in range(B_KV_HEADS)])
        for bb in range(h1_ref.shape[0]):
            attend_row(bb, bounded, ref_terms if bounded else None)

    def attend_row(bb, bounded, ref_terms):
        if bounded:
            bias_rel, sink_term = ref_terms
        for nb in range(tm // w):
            if nb == 0:
                kband = jnp.concatenate([ksp_ref[bb], ksc_ref[bb, 0:w, :]], axis=0)
            else:
                kband = ksc_ref[bb, (nb - 1) * w:(nb + 1) * w, :]
            for g in range(B_KV_HEADS):
                if nb == 0:
                    vband = jnp.concatenate([vtp_ref[bb, g], vtc_ref[bb, g, :, 0:w]], axis=1)
                else:
                    vband = vtc_ref[bb, g, :, (nb - 1) * w:(nb + 1) * w]
                qs = jnp.concatenate([qb_ref[bb, B_GROUP * g + hh, nb * w:(nb + 1) * w, :]
                                      for hh in range(B_GROUP)], axis=0)
                s = lax.dot_general(kband, qs, NT_DIMS, preferred_element_type=F32)
                before_start = first_tile & prev_rows
                if bounded:
                    e = jnp.exp2(s + bias_rel[g])
                    if nb == 0:
                        e = jnp.where(before_start, 0.0, e)
                    den = jnp.sum(e, axis=0, keepdims=True) + sink_term[g]
                else:
                    s = s + bias_ref[g]
                    if nb == 0:
                        s = jnp.where(before_start, NEG, s)
                    sink = sink_ref[g]
                    m = jnp.maximum(jnp.max(s, axis=0, keepdims=True), sink)
                    e = jnp.exp2(s - m)
                    den = jnp.sum(e, axis=0, keepdims=True) + jnp.exp2(sink - m)
                o_t = jnp.dot(vband, e.astype(BF16), preferred_element_type=F32) * (1.0 / den)
                for pr in range(B_GROUP // 2):
                    pair = jnp.concatenate([o_t[:, (2 * pr) * w:(2 * pr + 1) * w],
                                            o_t[:, (2 * pr + 1) * w:(2 * pr + 2) * w]], axis=0)
                    col = (B_GROUP * g + 2 * pr) * B_HEAD_DIM
                    o_scr[bb, nb * w:(nb + 1) * w, col:col + 2 * B_HEAD_DIM] = pair.T

        og = (o_scr[bb] * _silu(zb_ref[bb].astype(F32))).astype(BF16)
        out_ref[bb] = h1_ref[bb] + mod_ref[bb, 7:8, :] * jnp.dot(og, wo_ref[...], preferred_element_type=F32)

    @pl.when(bound <= BOUND_MAX)
    def _():
        attend(True)

    @pl.when(bound > BOUND_MAX)
    def _():
        attend(False)


def _final(bound, h1, qb, ks, vst, zb, wo, mod, bias, sink):
    bsz, seq, d = h1.shape
    nt = seq // TM
    nblk = TM // WINDOW
    nr = _rows_per_step(bsz)
    tok = lambda: pl.BlockSpec((nr, TM, d), lambda b, t: (b, t, 0))
    prev = lambda t: jnp.maximum(t * nblk - 1, 0)
    return pl.pallas_call(
        _final_kernel,
        grid=(bsz // nr, nt),
        in_specs=[pl.BlockSpec(memory_space=pltpu.SMEM),
                  tok(),
                  pl.BlockSpec((nr, B_HEADS, TM, LANES), lambda b, t: (b, 0, t, 0)),
                  pl.BlockSpec((nr, TM, LANES), lambda b, t: (b, t, 0)),
                  pl.BlockSpec((nr, WINDOW, LANES), lambda b, t: (b, prev(t), 0)),
                  pl.BlockSpec((nr, B_KV_HEADS, B_HEAD_DIM, TM), lambda b, t: (b, 0, 0, t)),
                  pl.BlockSpec((nr, B_KV_HEADS, B_HEAD_DIM, WINDOW), lambda b, t: (b, 0, 0, prev(t))),
                  tok(),
                  _const_spec(wo.shape),
                  pl.BlockSpec((nr, 8, d), lambda b, t: (b, 0, 0)),
                  _const_spec(bias.shape), _const_spec(sink.shape)],
        out_specs=tok(),
        out_shape=jax.ShapeDtypeStruct((bsz, seq, d), F32),
        scratch_shapes=[pltpu.VMEM((nr, TM, d), F32)],
        compiler_params=_params(),
        name="attn_b_out",
    )(bound, h1, qb, ks, ks, vst, vst, zb, wo, mod, bias, sink)


def _logit_bound(gq, gk, head_dim):
    return (head_dim * BOUND_SLACK * jnp.max(jnp.abs(gq)) * jnp.max(jnp.abs(gk))).reshape(1)


def _pair_gain(g, scale):
    return jnp.tile(g.astype(F32) * scale, 2).reshape(1, LANES)


def _swa_bias_table():
    slopes = np.asarray(_alibi_slopes_log2(B_HEADS), dtype=np.float32)
    k = np.arange(2 * WINDOW)[:, None]
    q = np.arange(WINDOW)[None, :]
    dist = q + WINDOW - k
    valid = (dist >= 0) & (dist < WINDOW)
    tab = np.empty((B_KV_HEADS, 2 * WINDOW, B_GROUP * WINDOW), np.float32)
    for g in range(B_KV_HEADS):
        for hh in range(B_GROUP):
            blk = np.where(valid, -slopes[B_GROUP * g + hh] * dist.astype(np.float32), np.float32(NEG))
            tab[g, :, hh * WINDOW:(hh + 1) * WINDOW] = blk
    return jnp.asarray(tab)


def kernel(x, c, a_norm_g, a_ada_w, a_ada_b, a_w_in, a_q_norm_g, a_k_norm_g, a_lambda_q1, a_lambda_k1,
           a_lambda_q2, a_lambda_k2, a_subln_g, a_w_out, kv_norm_g, kv_ada_w, kv_ada_b, w_kv, kv_k_norm_g,
           b_norm_g, b_ada_w, b_ada_b, b_w_in, b_q_norm_g, b_sinks, b_w_out):
    bsz, seq, d = x.shape
    assert a_norm_g.shape[0] == 1 and b_norm_g.shape[0] == 1, "one layer of each mixer"
    assert TM == TK and seq % TM == 0 and seq % TQ == 0 and TQ % TK == 0 and TM % WINDOW == 0
    a_inner = A_HEADS * 2 * A_HEAD_DIM
    b_inner = B_HEADS * B_HEAD_DIM

    mod = _modulation(c, a_ada_w[0], a_ada_b[0], kv_ada_w, kv_ada_b, b_ada_w[0], b_ada_b[0])

    w_in = a_w_in[0]
    wqk = w_in[:, 0:2 * a_inner].astype(BF16)
    wvt = w_in[:, 2 * a_inner:3 * a_inner].T.astype(BF16)
    wz = w_in[:, 3 * a_inner:].astype(BF16)
    gq = _pair_gain(a_q_norm_g[0], A_HEAD_DIM ** -0.5 * LOG2E)
    gk = _pair_gain(a_k_norm_g[0], 1.0)
    bound_a = _logit_bound(gq, gk, A_HEAD_DIM)

    qp, kp, vt, z = _proj_a(x, mod, a_norm_g[0].reshape(1, d), wqk, wz, wvt, gq, gk)

    lambda_init = 0.8 - 0.6 * math.exp(-0.3 * 0)
    row = lambda v: v.astype(F32).reshape(1, -1)
    og = _attn_a(bound_a, qp, kp, vt, z, row(a_lambda_q1[0]), row(a_lambda_k1[0]), row(a_lambda_q2[0]),
                 row(a_lambda_k2[0]), row(a_subln_g[0]), lambda_init)

    kv_w = B_KV_HEADS * B_HEAD_DIM
    wks = w_kv[:, 0:kv_w].astype(BF16)
    wvst = w_kv[:, kv_w:].T.astype(BF16)
    wqb = b_w_in[0][:, 0:b_inner].astype(BF16)
    wzb = b_w_in[0][:, b_inner:].astype(BF16)
    gks = _pair_gain(kv_k_norm_g, 1.0)
    gqb = _pair_gain(b_q_norm_g[0], B_HEAD_DIM ** -0.5 * LOG2E)

    h1, qb, ks, vst, zb = _mid(x, og, mod, a_w_out[0].astype(BF16), kv_norm_g.reshape(1, d),
                               b_norm_g[0].reshape(1, d), wks, wvst, wqb, wzb, gks, gqb)

    sink = jnp.repeat(b_sinks[0].astype(F32).reshape(B_KV_HEADS, B_GROUP) * LOG2E, WINDOW, axis=1)
    sink = sink.reshape(B_KV_HEADS, 1, B_GROUP * WINDOW)
    bound_b = _logit_bound(gqb, gks, B_HEAD_DIM)
    return _final(bound_b, h1, qb, ks, vst, zb, b_w_out[0].astype(BF16), mod, _swa_bias_table(), sink)
```

```python
import functools
import math

import numpy as np
import jax
import jax.numpy as jnp
from jax import lax
from jax.experimental import pallas as pl
from jax.experimental.pallas import tpu as pltpu

EPS = 1e-6
LANES = 128
A_HEADS = 8
A_HEAD_DIM = 64
B_HEADS = 16
B_KV_HEADS = 2
B_GROUP = B_HEADS // B_KV_HEADS
B_HEAD_DIM = 64
WINDOW = 128
NEG = -1e30
V_ROWS = 2 * A_HEAD_DIM

TM = 512
TQ = 1024
TK = 512
A_BATCH_ROWS = 2
VMEM_LIMIT = 56 * 1024 * 1024

LOG2E = 1.4426950408889634
EXP2_ZERO = 152.0
BOUND_MAX = 50.0
BOUND_SLACK = 1.01

F32 = jnp.float32
BF16 = jnp.bfloat16
NT_DIMS = (((1,), (1,)), ((), ()))


def _alibi_slopes(n_heads):
    return [2.0 ** (-8.0 * (h + 1) / n_heads) for h in range(n_heads)]


def _alibi_slopes_log2(n_heads):
    return [s * LOG2E for s in _alibi_slopes(n_heads)]


def _silu(x):
    return x * (1.0 / (1.0 + jnp.exp(-x)))


def _const_spec(shape):
    zeros = (0,) * len(shape)
    return pl.BlockSpec(shape, lambda *_: zeros, pipeline_mode=pl.Buffered(1))


def _rows_per_step(bsz):
    return A_BATCH_ROWS if bsz % A_BATCH_ROWS == 0 else 1


def _params():
    return pltpu.CompilerParams(dimension_semantics=("arbitrary",) * 2, vmem_limit_bytes=VMEM_LIMIT)


def _mod_kernel(c_ref, aw_ref, kw_ref, bw_ref, ab_ref, kb_ref, bb_ref, o_ref):
    j = pl.program_id(0)
    sc = _silu(c_ref[...])

    def emit(w_ref, b_ref):
        o_ref[...] = jnp.dot(sc, w_ref[...], precision=lax.Precision.HIGHEST,
                             preferred_element_type=F32) + b_ref[...]

    @pl.when(j < 3)
    def _():
        emit(aw_ref, ab_ref)

    @pl.when((j >= 3) & (j < 5))
    def _():
        emit(kw_ref, kb_ref)

    @pl.when(j >= 5)
    def _():
        emit(bw_ref, bb_ref)


def _modulation(c, a_w, a_b, kv_w, kv_b, b_w, b_b):
    bsz, d = c.shape
    wspec = lambda f: pl.BlockSpec((d, d), lambda j: (0, f(j)))
    bspec = lambda f: pl.BlockSpec((1, d), lambda j: (0, f(j)))
    fa = lambda j: jnp.minimum(j, 2)
    fk = lambda j: jnp.clip(j - 3, 0, 1)
    fb = lambda j: jnp.clip(j - 5, 0, 2)
    out = pl.pallas_call(
        _mod_kernel,
        grid=(8,),
        in_specs=[pl.BlockSpec((bsz, d), lambda j: (0, 0)),
                  wspec(fa), wspec(fk), wspec(fb), bspec(fa), bspec(fk), bspec(fb)],
        out_specs=pl.BlockSpec((bsz, d), lambda j: (0, j)),
        out_shape=jax.ShapeDtypeStruct((bsz, 8 * d), F32),
        compiler_params=pltpu.CompilerParams(dimension_semantics=("arbitrary",),
                                             vmem_limit_bytes=VMEM_LIMIT),
        name="ada_modulation",
    )(c, a_w, kv_w, b_w, a_b.reshape(1, -1), kv_b.reshape(1, -1), b_b.reshape(1, -1))
    return out.reshape(bsz, 8, d)


def _modulated_norm(x, g, scale, shift):
    ms = jnp.mean(x * x, axis=-1, keepdims=True)
    return (x * lax.rsqrt(ms + EPS) * g) * (1.0 + scale) + shift


def _pair_norm(x, gain2, first_half):
    y = x * x
    ss_a = jnp.sum(jnp.where(first_half, y, 0.0), axis=-1, keepdims=True)
    ss_b = jnp.sum(jnp.where(first_half, 0.0, y), axis=-1, keepdims=True)
    ms = jnp.where(first_half, ss_a, ss_b) * (2.0 / LANES)
    return x * lax.rsqrt(ms + EPS) * gain2


def _split_pair(xn, first_half, fill_a, fill_b):
    return (jnp.where(first_half, xn, fill_a),
            jnp.where(first_half, pltpu.roll(xn, LANES // 2, axis=1), fill_b))


def _proj_a_kernel(x_ref, mod_ref, g_ref, wqk_ref, wz_ref, wvt_ref, gq_ref, gk_ref,
                   qp_ref, kp_ref, vt_ref, z_ref, *, slopes):
    tm = x_ref.shape[1]
    lane = lax.broadcasted_iota(jnp.int32, (tm, LANES), 1)
    first_half = lane < A_HEAD_DIM
    col_hi = lane == A_HEAD_DIM
    col_lo = lane == A_HEAD_DIM + 1
    pos = lax.broadcasted_iota(jnp.int32, (tm, LANES), 0).astype(F32)
    q_fill = jnp.where(col_hi | col_lo, 1.0, 0.0)
    gq = gq_ref[...]
    gk = gk_ref[...]
    a_inner = A_HEADS * 2 * A_HEAD_DIM
    for bb in range(x_ref.shape[0]):
        u = _modulated_norm(x_ref[bb], g_ref[...], mod_ref[bb, 1:2, :], mod_ref[bb, 0:1, :]).astype(BF16)
        for hp in range(A_HEADS // 2):
            c0 = hp * 2 * LANES
            qq = jnp.dot(u, wqk_ref[:, c0:c0 + 2 * LANES], preferred_element_type=F32)
            kk = jnp.dot(u, wqk_ref[:, a_inner + c0:a_inner + c0 + 2 * LANES],
                         preferred_element_type=F32)
            for hl in range(2):
                h = 2 * hp + hl
                bias = pos * slopes[h]
                bias_hi = bias.astype(BF16).astype(F32)
                k_fill = jnp.where(col_hi, bias_hi, jnp.where(col_lo, bias - bias_hi, 0.0))
                qn = _pair_norm(qq[:, hl * LANES:(hl + 1) * LANES], gq, first_half)
                kn = _pair_norm(kk[:, hl * LANES:(hl + 1) * LANES], gk, first_half)
                for s, (qs, ks) in enumerate(zip(_split_pair(qn, first_half, q_fill, q_fill),
                                                 _split_pair(kn, first_half, k_fill, k_fill))):
                    qp_ref[bb, 2 * h + s] = qs.astype(BF16)
                    kp_ref[bb, 2 * h + s] = ks.astype(BF16)

        z_ref[bb] = jnp.dot(u, wz_ref[...], preferred_element_type=F32).astype(BF16)

        v_t = lax.dot_general(wvt_ref[...], u, NT_DIMS, preferred_element_type=F32)
        for h in range(A_HEADS):
            vt_ref[bb, h, 0] = v_t[h * V_ROWS:(h + 1) * V_ROWS].astype(BF16)


def _proj_a(x, mod, g, wqk, wz, wvt, gq, gk):
    bsz, seq, d = x.shape
    nt = seq // TM
    n_sub = 2 * A_HEADS
    kern = functools.partial(_proj_a_kernel, slopes=_alibi_slopes_log2(A_HEADS))
    nr = _rows_per_step(bsz)
    return pl.pallas_call(
        kern,
        grid=(bsz // nr, nt),
        in_specs=[pl.BlockSpec((nr, TM, d), lambda b, t: (b, t, 0)),
                  pl.BlockSpec((nr, 8, d), lambda b, t: (b, 0, 0)),
                  _const_spec((1, d)),
                  _const_spec(wqk.shape), _const_spec(wz.shape), _const_spec(wvt.shape),
                  _const_spec((1, LANES)), _const_spec((1, LANES))],
        out_specs=[pl.BlockSpec((nr, n_sub, TM, LANES), lambda b, t: (b, 0, t, 0)),
                   pl.BlockSpec((nr, n_sub, TM, LANES), lambda b, t: (b, 0, t, 0)),
                   pl.BlockSpec((nr, A_HEADS, 1, V_ROWS, TM), lambda b, t: (b, 0, t, 0, 0)),
                   pl.BlockSpec((nr, TM, d), lambda b, t: (b, t, 0))],
        out_shape=[jax.ShapeDtypeStruct((bsz, n_sub, seq, LANES), BF16),
                   jax.ShapeDtypeStruct((bsz, n_sub, seq, LANES), BF16),
                   jax.ShapeDtypeStruct((bsz, A_HEADS, nt, V_ROWS, TM), BF16),
                   jax.ShapeDtypeStruct((bsz, seq, d), BF16)],
        compiler_params=_params(),
        name="proj_a",
    )(x, mod, g, wqk, wz, wvt, gq, gk)


def _attn_a_kernel(bound_ref, qp_ref, kp_ref, vt_ref, z_ref, lq1_ref, lk1_ref, lq2_ref, lk2_ref, gs_ref,
                   og_ref, acc_ref, l_ref, m_ref, p_ref, qt_ref, *, slopes2, lambda_init):
    h = pl.program_id(1)
    i = pl.program_id(2)
    nb = qp_ref.shape[0]
    tq = qp_ref.shape[2]
    tk = kp_ref.shape[3]
    vd = 2 * A_HEAD_DIM

    slope2 = jnp.float32(slopes2[0])
    zero_dist = jnp.int32(math.ceil(EXP2_ZERO / slopes2[0]))
    for hh in range(1, A_HEADS):
        slope2 = jnp.where(h == hh, jnp.float32(slopes2[hh]), slope2)
        zero_dist = jnp.where(h == hh, jnp.int32(math.ceil(EXP2_ZERO / slopes2[hh])), zero_dist)

    bound = bound_ref[0]

    def tile_offset(j):
        return slope2 * (j * tk - i * tq).astype(F32)

    n_diag = tq // tk

    def causal_mask(width):
        krow = lax.broadcasted_iota(jnp.int32, (tk, width), 0)
        qcol = lax.broadcasted_iota(jnp.int32, (tk, width), 1)
        return krow <= qcol

    @pl.when(bound <= BOUND_MAX)
    def _():
        q_local = lax.broadcasted_iota(jnp.int32, (1, tq), 1).astype(F32)
        ref = bound + slope2 * q_local

        acc_ref[...] = jnp.zeros(acc_ref.shape, F32)
        l_ref[...] = jnp.zeros(l_ref.shape, F32)
        for bb in range(nb):
            for sub in range(2):
                qt_ref[bb, sub] = qp_ref[bb, sub].T

        def probs(j, bb, sub, q0):
            s = jnp.dot(kp_ref[bb, sub, j], qt_ref[bb, sub, :, q0:],
                        preferred_element_type=F32)
            return jnp.exp2(s + (tile_offset(j) - ref[:, q0:]))

        def stage_a(j):
            for bb in range(nb):
                for sub in range(2):
                    e = probs(j, bb, sub, 0)
                    l_ref[bb, sub] += jnp.sum(e, axis=0, keepdims=True)
                    p_ref[bb, sub] = e.astype(BF16)

        def stage_b(j):
            for bb in range(nb):
                v_t = vt_ref[bb, 0, j]
                for sub in range(2):
                    acc_ref[bb, sub] += jnp.dot(v_t, p_ref[bb, sub], preferred_element_type=F32)

        def diag_tile(r):
            q0 = r * tk
            j = i * n_diag + r
            for bb in range(nb):
                v_t = vt_ref[bb, 0, j]
                for sub in range(2):
                    e = jnp.where(causal_mask(tq - q0), probs(j, bb, sub, q0), 0.0)
                    l_ref[bb, sub, :, q0:] += jnp.sum(e, axis=0, keepdims=True)
                    acc_ref[bb, sub, :, q0:] += jnp.dot(v_t, e.astype(BF16), preferred_element_type=F32)

        first = jnp.maximum(i * tq - zero_dist + 1, 0) // tk
        last = i * n_diag

        @pl.when(first < last)
        def _():
            stage_a(first)

        @pl.when(first >= last)
        def _():
            p_ref[...] = jnp.zeros(p_ref.shape, BF16)

        def body(j, carry):
            stage_b(j - 1)
            stage_a(j)
            return carry

        lax.fori_loop(first + 1, last, body, 0)
        stage_b(jnp.maximum(last - 1, 0))
        for r in range(n_diag):
            diag_tile(r)

    @pl.when(bound > BOUND_MAX)
    def _():
        acc_ref[...] = jnp.zeros(acc_ref.shape, F32)
        l_ref[...] = jnp.zeros(l_ref.shape, F32)
        m_ref[...] = jnp.full(m_ref.shape, NEG, F32)

        def kv_tile(j, diag=None):
            q0 = 0 if diag is None else diag * tk
            c = tile_offset(j)
            for bb in range(nb):
                v_t = vt_ref[bb, 0, j]
                for sub in range(2):
                    s = lax.dot_general(kp_ref[bb, sub, j], qp_ref[bb, sub, q0:, :], NT_DIMS,
                                        preferred_element_type=F32)
                    if diag is not None:
                        s = jnp.where(causal_mask(tq - q0), s, NEG)
                    m_old = m_ref[bb, sub, :, q0:]
                    m_new = jnp.maximum(m_old, jnp.max(s, axis=0, keepdims=True) + c)
                    alpha = jnp.exp2(m_old - m_new)
                    p = jnp.exp2(s - (m_new - c))
                    pv = jnp.dot(v_t, p.astype(BF16), preferred_element_type=F32)
                    acc_ref[bb, sub, :, q0:] = acc_ref[bb, sub, :, q0:] * alpha + pv
                    l_ref[bb, sub, :, q0:] = (l_ref[bb, sub, :, q0:] * alpha
                                              + jnp.sum(p, axis=0, keepdims=True))
                    m_ref[bb, sub, :, q0:] = m_new

        def body(j, carry):
            kv_tile(j)
            return carry

        lax.fori_loop(0, i * n_diag, body, 0)
        for r in range(n_diag):
            kv_tile(i * n_diag + r, diag=r)

    lam = (jnp.exp(jnp.sum(lq1_ref[...] * lk1_ref[...], axis=-1, keepdims=True))
           - jnp.exp(jnp.sum(lq2_ref[...] * lk2_ref[...], axis=-1, keepdims=True))
           + lambda_init)
    for bb in range(nb):
        o_t = (acc_ref[bb, 0] * (1.0 / l_ref[bb, 0])
               - lam * (acc_ref[bb, 1] * (1.0 / l_ref[bb, 1])))
        ms = jnp.mean(o_t * o_t, axis=0, keepdims=True)
        o = (o_t * lax.rsqrt(ms + EPS)).T
        o = o * gs_ref[...] * (1.0 - lambda_init)
        og_ref[bb] = (o * _silu(z_ref[bb].astype(F32))).astype(BF16)


def _attn_a(bound, qp, kp, vt, z, lq1, lk1, lq2, lk2, gs, lambda_init):
    bsz, n_sub, seq, _ = qp.shape
    nk = seq // TK
    nq = seq // TQ
    d = z.shape[-1]
    vd = 2 * A_HEAD_DIM
    kp5 = kp.reshape(bsz, n_sub, nk, TK, LANES)
    nb = A_BATCH_ROWS if bsz % A_BATCH_ROWS == 0 else 1
    kern = functools.partial(_attn_a_kernel, slopes2=_alibi_slopes_log2(A_HEADS), lambda_init=lambda_init)
    vec = lambda n: pl.BlockSpec((1, n), lambda b, h, i: (0, 0))
    return pl.pallas_call(
        kern,
        grid=(bsz // nb, A_HEADS, nq),
        in_specs=[pl.BlockSpec(memory_space=pltpu.SMEM),
                  pl.BlockSpec((nb, 2, TQ, LANES), lambda b, h, i: (b, h, i, 0)),
                  pl.BlockSpec((nb, 2, nk, TK, LANES), lambda b, h, i: (b, h, 0, 0, 0)),
                  pl.BlockSpec((nb, 1, nk, V_ROWS, TK), lambda b, h, i: (b, h, 0, 0, 0)),
                  pl.BlockSpec((nb, TQ, vd), lambda b, h, i: (b, i, h)),
                  vec(A_HEAD_DIM), vec(A_HEAD_DIM), vec(A_HEAD_DIM), vec(A_HEAD_DIM), vec(vd)],
        out_specs=pl.BlockSpec((nb, TQ, vd), lambda b, h, i: (b, i, h)),
        out_shape=jax.ShapeDtypeStruct((bsz, seq, d), BF16),
        scratch_shapes=[pltpu.VMEM((nb, 2, V_ROWS, TQ), F32), pltpu.VMEM((nb, 2, 1, TQ), F32),
                        pltpu.VMEM((nb, 2, 1, TQ), F32), pltpu.VMEM((nb, 2, TK, TQ), BF16),
                        pltpu.VMEM((nb, 2, LANES, TQ), BF16)],
        compiler_params=pltpu.CompilerParams(dimension_semantics=("arbitrary",) * 3,
                                             vmem_limit_bytes=VMEM_LIMIT),
        name="attn_a",
    )(bound, qp, kp5, vt, z, lq1, lk1, lq2, lk2, gs)


def _mid_kernel(x_ref, og_ref, mod_ref, wo_ref, gkv_ref, gb_ref, wk_ref, wvt_ref, wqb_ref, wzb_ref,
                gks_ref, gqb_ref, h1_ref, qb_ref, ks_ref, vst_ref, zb_ref):
    tm = x_ref.shape[1]
    first_half = lax.broadcasted_iota(jnp.int32, (tm, LANES), 1) < B_HEAD_DIM
    gqb = gqb_ref[...]
    for bb in range(x_ref.shape[0]):
        gate = mod_ref[bb, 2:3, :]
        h1 = x_ref[bb] + gate * jnp.dot(og_ref[bb], wo_ref[...], preferred_element_type=F32)
        h1_ref[bb] = h1

        ms = jnp.mean(h1 * h1, axis=-1, keepdims=True)
        r = h1 * lax.rsqrt(ms + EPS)
        ukv = ((r * gkv_ref[...]) * (1.0 + mod_ref[bb, 4:5, :]) + mod_ref[bb, 3:4, :]).astype(BF16)
        ub = ((r * gb_ref[...]) * (1.0 + mod_ref[bb, 6:7, :]) + mod_ref[bb, 5:6, :]).astype(BF16)

        kk = jnp.dot(ukv, wk_ref[...], preferred_element_type=F32)
        ks_ref[bb] = _pair_norm(kk, gks_ref[...], first_half).astype(BF16)
        v_t = lax.dot_general(wvt_ref[...], ukv, NT_DIMS, preferred_element_type=F32)
        for g in range(B_KV_HEADS):
            vst_ref[bb, g] = v_t[g * B_HEAD_DIM:(g + 1) * B_HEAD_DIM].astype(BF16)

        for pp in range(B_HEADS // 4):
            qq = jnp.dot(ub, wqb_ref[:, pp * 2 * LANES:(pp + 1) * 2 * LANES], preferred_element_type=F32)
            for hl in range(2):
                h0 = 4 * pp + 2 * hl
                qn = _pair_norm(qq[:, hl * LANES:(hl + 1) * LANES], gqb, first_half)
                swapped = pltpu.roll(qn, LANES // 2, axis=1)
                if h0 < B_GROUP:
                    qb_ref[bb, h0] = jnp.where(first_half, qn, 0.0).astype(BF16)
                    qb_ref[bb, h0 + 1] = jnp.where(first_half, swapped, 0.0).astype(BF16)
                else:
                    qb_ref[bb, h0] = jnp.where(first_half, 0.0, swapped).astype(BF16)
                    qb_ref[bb, h0 + 1] = jnp.where(first_half, 0.0, qn).astype(BF16)
        zb_ref[bb] = jnp.dot(ub, wzb_ref[...], preferred_element_type=F32).astype(BF16)


def _mid(x, og, mod, wo, gkv, gb, wk, wvt, wqb, wzb, gks, gqb):
    bsz, seq, d = x.shape
    nt = seq // TM
    nr = _rows_per_step(bsz)
    tok = lambda: pl.BlockSpec((nr, TM, d), lambda b, t: (b, t, 0))
    return pl.pallas_call(
        _mid_kernel,
        grid=(bsz // nr, nt),
        in_specs=[tok(), tok(),
                  pl.BlockSpec((nr, 8, d), lambda b, t: (b, 0, 0)),
                  _const_spec(wo.shape), _const_spec((1, d)), _const_spec((1, d)),
                  _const_spec(wk.shape), _const_spec(wvt.shape), _const_spec(wqb.shape),
                  _const_spec(wzb.shape), _const_spec((1, LANES)), _const_spec((1, LANES))],
        out_specs=[tok(),
                   pl.BlockSpec((nr, B_HEADS, TM, LANES), lambda b, t: (b, 0, t, 0)),
                   pl.BlockSpec((nr, TM, LANES), lambda b, t: (b, t, 0)),
                   pl.BlockSpec((nr, B_KV_HEADS, B_HEAD_DIM, TM), lambda b, t: (b, 0, 0, t)),
                   tok()],
        out_shape=[jax.ShapeDtypeStruct((bsz, seq, d), F32),
                   jax.ShapeDtypeStruct((bsz, B_HEADS, seq, LANES), BF16),
                   jax.ShapeDtypeStruct((bsz, seq, LANES), BF16),
                   jax.ShapeDtypeStruct((bsz, B_KV_HEADS, B_HEAD_DIM, seq), BF16),
                   jax.ShapeDtypeStruct((bsz, seq, d), BF16)],
        compiler_params=_params(),
        name="mid_proj",
    )(x, og, mod, wo, gkv, gb, wk, wvt, wqb, wzb, gks, gqb)


def _final_kernel(bound_ref, h1_ref, qb_ref, ksc_ref, ksp_ref, vtc_ref, vtp_ref, zb_ref, wo_ref, mod_ref,
                  bias_ref, sink_ref, out_ref, o_scr):
    t = pl.program_id(1)
    tm = h1_ref.shape[1]
    w = WINDOW
    first_tile = t == 0
    prev_rows = lax.broadcasted_iota(jnp.int32, (2 * w, B_GROUP * w), 0) < w
    bound = bound_ref[0]

    def attend(bounded):
        if bounded:
            ref = [jnp.maximum(sink_ref[g], bound) for g in range(B_KV_HEADS)]
            ref_terms = ([bias_ref[g] - ref[g] for g in range(B_KV_HEADS)],
                         [jnp.exp2(sink_ref[g] - ref[g]) for g in range(B_KV_HEADS)])
        for bb in range(h1_ref.shape[0]):
            attend_row(bb, bounded, ref_terms if bounded else None)

    def attend_row(bb, bounded, ref_terms):
        if bounded:
            bias_rel, sink_term = ref_terms
        for nb in range(tm // w):
            if nb == 0:
                kband = jnp.concatenate([ksp_ref[bb], ksc_ref[bb, 0:w, :]], axis=0)
            else:
                kband = ksc_ref[bb, (nb - 1) * w:(nb + 1) * w, :]
            for g in range(B_KV_HEADS):
                if nb == 0:
                    vband = jnp.concatenate([vtp_ref[bb, g], vtc_ref[bb, g, :, 0:w]], axis=1)
                else:
                    vband = vtc_ref[bb, g, :, (nb - 1) * w:(nb + 1) * w]
                qs = jnp.concatenate([qb_ref[bb, B_GROUP * g + hh, nb * w:(nb + 1) * w, :]
                                      for hh in range(B_GROUP)], axis=0)
                s = lax.dot_general(kband, qs, NT_DIMS, preferred_element_type=F32)
                before_start = first_tile & prev_rows
                if bounded:
                    e = jnp.exp2(s + bias_rel[g])
                    if nb == 0:
                        e = jnp.where(before_start, 0.0, e)
                    den = jnp.sum(e, axis=0, keepdims=True) + sink_term[g]
                else:
                    s = s + bias_ref[g]
                    if nb == 0:
                        s = jnp.where(before_start, NEG, s)
                    sink = sink_ref[g]
                    m = jnp.maximum(jnp.max(s, axis=0, keepdims=True), sink)
                    e = jnp.exp2(s - m)
                    den = jnp.sum(e, axis=0, keepdims=True) + jnp.exp2(sink - m)
                o_t = jnp.dot(vband, e.astype(BF16), preferred_element_type=F32) * (1.0 / den)
                for pr in range(B_GROUP // 2):
                    pair = jnp.concatenate([o_t[:, (2 * pr) * w:(2 * pr + 1) * w],
                                            o_t[:, (2 * pr + 1) * w:(2 * pr + 2) * w]], axis=0)
                    col = (B_GROUP * g + 2 * pr) * B_HEAD_DIM
                    o_scr[bb, nb * w:(nb + 1) * w, col:col + 2 * B_HEAD_DIM] = pair.T

        og = (o_scr[bb] * _silu(zb_ref[bb].astype(F32))).astype(BF16)
        out_ref[bb] = h1_ref[bb] + mod_ref[bb, 7:8, :] * jnp.dot(og, wo_ref[...], preferred_element_type=F32)

    @pl.when(bound <= BOUND_MAX)
    def _():
        attend(True)

    @pl.when(bound > BOUND_MAX)
    def _():
        attend(False)


def _final(bound, h1, qb, ks, vst, zb, wo, mod, bias, sink):
    bsz, seq, d = h1.shape
    nt = seq // TM
    nblk = TM // WINDOW
    nr = _rows_per_step(bsz)
    tok = lambda: pl.BlockSpec((nr, TM, d), lambda b, t: (b, t, 0))
    prev = lambda t: jnp.maximum(t * nblk - 1, 0)
    return pl.pallas_call(
        _final_kernel,
        grid=(bsz // nr, nt),
        in_specs=[pl.BlockSpec(memory_space=pltpu.SMEM),
                  tok(),
                  pl.BlockSpec((nr, B_HEADS, TM, LANES), lambda b, t: (b, 0, t, 0)),
                  pl.BlockSpec((nr, TM, LANES), lambda b, t: (b, t, 0)),
                  pl.BlockSpec((nr, WINDOW, LANES), lambda b, t: (b, prev(t), 0)),
                  pl.BlockSpec((nr, B_KV_HEADS, B_HEAD_DIM, TM), lambda b, t: (b, 0, 0, t)),
                  pl.BlockSpec((nr, B_KV_HEADS, B_HEAD_DIM, WINDOW), lambda b, t: (b, 0, 0, prev(t))),
                  tok(),
                  _const_spec(wo.shape),
                  pl.BlockSpec((nr, 8, d), lambda b, t: (b, 0, 0)),
                  _const_spec(bias.shape), _const_spec(sink.shape)],
        out_specs=tok(),
        out_shape=jax.ShapeDtypeStruct((bsz, seq, d), F32),
        scratch_shapes=[pltpu.VMEM((nr, TM, d), F32)],
        compiler_params=_params(),
        name="attn_b_out",
    )(bound, h1, qb, ks, ks, vst, vst, zb, wo, mod, bias, sink)


def _logit_bound(gq, gk, head_dim):
    return (head_dim * BOUND_SLACK * jnp.max(jnp.abs(gq)) * jnp.max(jnp.abs(gk))).reshape(1)


def _pair_gain(g, scale):
    return jnp.tile(g.astype(F32) * scale, 2).reshape(1, LANES)


def _swa_bias_table():
    slopes = np.asarray(_alibi_slopes_log2(B_HEADS), dtype=np.float32)
    k = np.arange(2 * WINDOW)[:, None]
    q = np.arange(WINDOW)[None, :]
    dist = q + WINDOW - k
    valid = (dist >= 0) & (dist < WINDOW)
    tab = np.empty((B_KV_HEADS, 2 * WINDOW, B_GROUP * WINDOW), np.float32)
    for g in range(B_KV_HEADS):
        for hh in range(B_GROUP):
            blk = np.where(valid, -slopes[B_GROUP * g + hh] * dist.astype(np.float32), np.float32(NEG))
            tab[g, :, hh * WINDOW:(hh + 1) * WINDOW] = blk
    return jnp.asarray(tab)


def kernel(x, c, a_norm_g, a_ada_w, a_ada_b, a_w_in, a_q_norm_g, a_k_norm_g, a_lambda_q1, a_lambda_k1,
           a_lambda_q2, a_lambda_k2, a_subln_g, a_w_out, kv_norm_g, kv_ada_w, kv_ada_b, w_kv, kv_k_norm_g,
           b_norm_g, b_ada_w, b_ada_b, b_w_in, b_q_norm_g, b_sinks, b_w_out):
    bsz, seq, d = x.shape
    assert a_norm_g.shape[0] == 1 and b_norm_g.shape[0] == 1, "one layer of each mixer"
    assert TM == TK and seq % TM == 0 and seq % TQ == 0 and TQ % TK == 0 and TM % WINDOW == 0
    a_inner = A_HEADS * 2 * A_HEAD_DIM
    b_inner = B_HEADS * B_HEAD_DIM

    mod = _modulation(c, a_ada_w[0], a_ada_b[0], kv_ada_w, kv_ada_b, b_ada_w[0], b_ada_b[0])

    w_in = a_w_in[0]
    wqk = w_in[:, 0:2 * a_inner].astype(BF16)
    wvt = w_in[:, 2 * a_inner:3 * a_inner].T.astype(BF16)
    wz = w_in[:, 3 * a_inner:].astype(BF16)
    gq = _pair_gain(a_q_norm_g[0], A_HEAD_DIM ** -0.5 * LOG2E)
    gk = _pair_gain(a_k_norm_g[0], 1.0)
    bound_a = _logit_bound(gq, gk, A_HEAD_DIM)

    qp, kp, vt, z = _proj_a(x, mod, a_norm_g[0].reshape(1, d), wqk, wz, wvt, gq, gk)

    lambda_init = 0.8 - 0.6 * math.exp(-0.3 * 0)
    row = lambda v: v.astype(F32).reshape(1, -1)
    og = _attn_a(bound_a, qp, kp, vt, z, row(a_lambda_q1[0]), row(a_lambda_k1[0]), row(a_lambda_q2[0]),
                 row(a_lambda_k2[0]), row(a_subln_g[0]), lambda_init)

    kv_w = B_KV_HEADS * B_HEAD_DIM
    wks = w_kv[:, 0:kv_w].astype(BF16)
    wvst = w_kv[:, kv_w:].T.astype(BF16)
    wqb = b_w_in[0][:, 0:b_inner].astype(BF16)
    wzb = b_w_in[0][:, b_inner:].astype(BF16)
    gks = _pair_gain(kv_k_norm_g, 1.0)
    gqb = _pair_gain(b_q_norm_g[0], B_HEAD_DIM ** -0.5 * LOG2E)

    h1, qb, ks, vst, zb = _mid(x, og, mod, a_w_out[0].astype(BF16), kv_norm_g.reshape(1, d),
                               b_norm_g[0].reshape(1, d), wks, wvst, wqb, wzb, gks, gqb)

    sink = jnp.repeat(b_sinks[0].astype(F32).reshape(B_KV_HEADS, B_GROUP) * LOG2E, WINDOW, axis=1)
    sink = sink.reshape(B_KV_HEADS, 1, B_GROUP * WINDOW)
    bound_b = _logit_bound(gqb, gks, B_HEAD_DIM)
    return _final(bound_b, h1, qb, ks, vst, zb, b_w_out[0].astype(BF16), mod, _swa_bias_table(), sink)
```

```python
import functools
import math

import numpy as np
import jax
import jax.numpy as jnp
from jax import lax
from jax.experimental import pallas as pl
from jax.experimental.pallas import tpu as pltpu

EPS = 1e-6
LANES = 128
A_HEADS = 8
A_HEAD_DIM = 64
B_HEADS = 16
B_KV_HEADS = 2
B_GROUP = B_HEADS // B_KV_HEADS
B_HEAD_DIM = 64
WINDOW = 128
NEG = -1e30
V_ROWS = 2 * A_HEAD_DIM

TM = 512
TQ = 1024
TK = 512
A_BATCH_ROWS = 2
VMEM_LIMIT = 56 * 1024 * 1024

LOG2E = 1.4426950408889634
EXP2_ZERO = 152.0
BOUND_MAX = 50.0
BOUND_SLACK = 1.01

F32 = jnp.float32
BF16 = jnp.bfloat16
NT_DIMS = (((1,), (1,)), ((), ()))


def _alibi_slopes(n_heads):
    return [2.0 ** (-8.0 * (h + 1) / n_heads) for h in range(n_heads)]


def _alibi_slopes_log2(n_heads):
    return [s * LOG2E for s in _alibi_slopes(n_heads)]


def _silu(x):
    return x * (1.0 / (1.0 + jnp.exp(-x)))


def _const_spec(shape):
    zeros = (0,) * len(shape)
    return pl.BlockSpec(shape, lambda *_: zeros, pipeline_mode=pl.Buffered(1))


def _rows_per_step(bsz):
    return A_BATCH_ROWS if bsz % A_BATCH_ROWS == 0 else 1


def _params():
    return pltpu.CompilerParams(dimension_semantics=("arbitrary",) * 2, vmem_limit_bytes=VMEM_LIMIT)


def _mod_kernel(c_ref, aw_ref, kw_ref, bw_ref, ab_ref, kb_ref, bb_ref, o_ref):
    j = pl.program_id(0)
    sc = _silu(c_ref[...])

    def emit(w_ref, b_ref):
        o_ref[...] = jnp.dot(sc, w_ref[...], precision=lax.Precision.HIGHEST,
                             preferred_element_type=F32) + b_ref[...]

    @pl.when(j < 3)
    def _():
        emit(aw_ref, ab_ref)

    @pl.when((j >= 3) & (j < 5))
    def _():
        emit(kw_ref, kb_ref)

    @pl.when(j >= 5)
    def _():
        emit(bw_ref, bb_ref)


def _modulation(c, a_w, a_b, kv_w, kv_b, b_w, b_b):
    bsz, d = c.shape
    wspec = lambda f: pl.BlockSpec((d, d), lambda j: (0, f(j)))
    bspec = lambda f: pl.BlockSpec((1, d), lambda j: (0, f(j)))
    fa = lambda j: jnp.minimum(j, 2)
    fk = lambda j: jnp.clip(j - 3, 0, 1)
    fb = lambda j: jnp.clip(j - 5, 0, 2)
    out = pl.pallas_call(
        _mod_kernel,
        grid=(8,),
        in_specs=[pl.BlockSpec((bsz, d), lambda j: (0, 0)),
                  wspec(fa), wspec(fk), wspec(fb), bspec(fa), bspec(fk), bspec(fb)],
        out_specs=pl.BlockSpec((bsz, d), lambda j: (0, j)),
        out_shape=jax.ShapeDtypeStruct((bsz, 8 * d), F32),
        compiler_params=pltpu.CompilerParams(dimension_semantics=("arbitrary",),
                                             vmem_limit_bytes=VMEM_LIMIT),
        name="ada_modulation",
    )(c, a_w, kv_w, b_w, a_b.reshape(1, -1), kv_b.reshape(1, -1), b_b.reshape(1, -1))
    return out.reshape(bsz, 8, d)


def _modulated_norm(x, g, scale, shift):
    ms = jnp.mean(x * x, axis=-1, keepdims=True)
    return (x * lax.rsqrt(ms + EPS) * g) * (1.0 + scale) + shift


def _pair_norm(x, gain2, first_half):
    y = x * x
    ss_a = jnp.sum(jnp.where(first_half, y, 0.0), axis=-1, keepdims=True)
    ss_b = jnp.sum(jnp.where(first_half, 0.0, y), axis=-1, keepdims=True)
    ms = jnp.where(first_half, ss_a, ss_b) * (2.0 / LANES)
    return x * lax.rsqrt(ms + EPS) * gain2


def _split_pair(xn, first_half, fill_a, fill_b):
    return (jnp.where(first_half, xn, fill_a),
            jnp.where(first_half, pltpu.roll(xn, LANES // 2, axis=1), fill_b))


def _proj_a_kernel(x_ref, mod_ref, g_ref, wqk_ref, wz_ref, wvt_ref, gq_ref, gk_ref,
                   qp_ref, kp_ref, vt_ref, z_ref, *, slopes):
    tm = x_ref.shape[1]
    lane = lax.broadcasted_iota(jnp.int32, (tm, LANES), 1)
    first_half = lane < A_HEAD_DIM
    col_hi = lane == A_HEAD_DIM
    col_lo = lane == A_HEAD_DIM + 1
    pos = lax.broadcasted_iota(jnp.int32, (tm, LANES), 0).astype(F32)
    q_fill = jnp.where(col_hi | col_lo, 1.0, 0.0)
    gq = gq_ref[...]
    gk = gk_ref[...]
    a_inner = A_HEADS * 2 * A_HEAD_DIM
    for bb in range(x_ref.shape[0]):
        u = _modulated_norm(x_ref[bb], g_ref[...], mod_ref[bb, 1:2, :], mod_ref[bb, 0:1, :]).astype(BF16)
        for hp in range(A_HEADS // 2):
            c0 = hp * 2 * LANES
            qq = jnp.dot(u, wqk_ref[:, c0:c0 + 2 * LANES], preferred_element_type=F32)
            kk = jnp.dot(u, wqk_ref[:, a_inner + c0:a_inner + c0 + 2 * LANES],
                         preferred_element_type=F32)
            for hl in range(2):
                h = 2 * hp + hl
                bias = pos * slopes[h]
                bias_hi = bias.astype(BF16).astype(F32)
                k_fill = jnp.where(col_hi, bias_hi, jnp.where(col_lo, bias - bias_hi, 0.0))
                qn = _pair_norm(qq[:, hl * LANES:(hl + 1) * LANES], gq, first_half)
                kn = _pair_norm(kk[:, hl * LANES:(hl + 1) * LANES], gk, first_half)
                for s, (qs, ks) in enumerate(zip(_split_pair(qn, first_half, q_fill, q_fill),
                                                 _split_pair(kn, first_half, k_fill, k_fill))):
                    qp_ref[bb, 2 * h + s] = qs.astype(BF16)
                    kp_ref[bb, 2 * h + s] = ks.astype(BF16)

        z_ref[bb] = jnp.dot(u, wz_ref[...], preferred_element_type=F32).astype(BF16)

        v_t = lax.dot_general(wvt_ref[...], u, NT_DIMS, preferred_element_type=F32)
        for h in range(A_HEADS):
            vt_ref[bb, h, 0] = v_t[h * V_ROWS:(h + 1) * V_ROWS].astype(BF16)


def _proj_a(x, mod, g, wqk, wz, wvt, gq, gk):
    bsz, seq, d = x.shape
    nt = seq // TM
    n_sub = 2 * A_HEADS
    kern = functools.partial(_proj_a_kernel, slopes=_alibi_slopes_log2(A_HEADS))
    nr = _rows_per_step(bsz)
    return pl.pallas_call(
        kern,
        grid=(bsz // nr, nt),
        in_specs=[pl.BlockSpec((nr, TM, d), lambda b, t: (b, t, 0)),
                  pl.BlockSpec((nr, 8, d), lambda b, t: (b, 0, 0)),
                  _const_spec((1, d)),
                  _const_spec(wqk.shape), _const_spec(wz.shape), _const_spec(wvt.shape),
                  _const_spec((1, LANES)), _const_spec((1, LANES))],
        out_specs=[pl.BlockSpec((nr, n_sub, TM, LANES), lambda b, t: (b, 0, t, 0)),
                   pl.BlockSpec((nr, n_sub, TM, LANES), lambda b, t: (b, 0, t, 0)),
                   pl.BlockSpec((nr, A_HEADS, 1, V_ROWS, TM), lambda b, t: (b, 0, t, 0, 0)),
                   pl.BlockSpec((nr, TM, d), lambda b, t: (b, t, 0))],
        out_shape=[jax.ShapeDtypeStruct((bsz, n_sub, seq, LANES), BF16),
                   jax.ShapeDtypeStruct((bsz, n_sub, seq, LANES), BF16),
                   jax.ShapeDtypeStruct((bsz, A_HEADS, nt, V_ROWS, TM), BF16),
                   jax.ShapeDtypeStruct((bsz, seq, d), BF16)],
        compiler_params=_params(),
        name="proj_a",
    )(x, mod, g, wqk, wz, wvt, gq, gk)


def _attn_a_kernel(bound_ref, qp_ref, kp_ref, vt_ref, z_ref, lq1_ref, lk1_ref, lq2_ref, lk2_ref, gs_ref,
                   og_ref, acc_ref, l_ref, m_ref, p_ref, *, slopes2, lambda_init):
    h = pl.program_id(1)
    i = pl.program_id(2)
    nb = qp_ref.shape[0]
    tq = qp_ref.shape[2]
    tk = kp_ref.shape[3]
    vd = 2 * A_HEAD_DIM

    slope2 = jnp.float32(slopes2[0])
    zero_dist = jnp.int32(math.ceil(EXP2_ZERO / slopes2[0]))
    for hh in range(1, A_HEADS):
        slope2 = jnp.where(h == hh, jnp.float32(slopes2[hh]), slope2)
        zero_dist = jnp.where(h == hh, jnp.int32(math.ceil(EXP2_ZERO / slopes2[hh])), zero_dist)

    bound = bound_ref[0]

    def tile_offset(j):
        return slope2 * (j * tk - i * tq).astype(F32)

    n_diag = tq // tk

    def causal_mask(width):
        krow = lax.broadcasted_iota(jnp.int32, (tk, width), 0)
        qcol = lax.broadcasted_iota(jnp.int32, (tk, width), 1)
        return krow <= qcol

    @pl.when(bound <= BOUND_MAX)
    def _():
        q_local = lax.broadcasted_iota(jnp.int32, (1, tq), 1).astype(F32)
        ref = bound + slope2 * q_local

        acc_ref[...] = jnp.zeros(acc_ref.shape, F32)
        l_ref[...] = jnp.zeros(l_ref.shape, F32)

        def probs(j, bb, sub, q0):
            s = lax.dot_general(kp_ref[bb, sub, j], qp_ref[bb, sub, q0:, :], NT_DIMS,
                                preferred_element_type=F32)
            return jnp.exp2(s + (tile_offset(j) - ref[:, q0:]))

        def stage_a(j):
            for bb in range(nb):
                for sub in range(2):
                    e = probs(j, bb, sub, 0)
                    l_ref[bb, sub] += jnp.sum(e, axis=0, keepdims=True)
                    p_ref[bb, sub] = e.astype(BF16)

        def stage_b(j):
            for bb in range(nb):
                v_t = vt_ref[bb, 0, j]
                for sub in range(2):
                    acc_ref[bb, sub] += jnp.dot(v_t, p_ref[bb, sub], preferred_element_type=F32)

        def diag_tile(r):
            q0 = r * tk
            j = i * n_diag + r
            for bb in range(nb):
                v_t = vt_ref[bb, 0, j]
                for sub in range(2):
                    e = jnp.where(causal_mask(tq - q0), probs(j, bb, sub, q0), 0.0)
                    l_ref[bb, sub, :, q0:] += jnp.sum(e, axis=0, keepdims=True)
                    acc_ref[bb, sub, :, q0:] += jnp.dot(v_t, e.astype(BF16), preferred_element_type=F32)

        first = jnp.maximum(i * tq - zero_dist + 1, 0) // tk
        last = i * n_diag

        @pl.when(first < last)
        def _():
            stage_a(first)

        @pl.when(first >= last)
        def _():
            p_ref[...] = jnp.zeros(p_ref.shape, BF16)

        odd = ((last - first) >= 2) & ((last - first) % 2 == 0)

        @pl.when(odd)
        def _():
            stage_b(first)
            stage_a(first + 1)

        start = first + 1 + odd.astype(jnp.int32)

        def body(t, carry):
            j = start + 2 * t
            stage_b(j - 1)
            stage_a(j)
            stage_b(j)
            stage_a(j + 1)
            return carry

        lax.fori_loop(0, jnp.maximum(last - start, 0) // 2, body, 0)
        stage_b(jnp.maximum(last - 1, 0))
        for r in range(n_diag):
            diag_tile(r)

    @pl.when(bound > BOUND_MAX)
    def _():
        acc_ref[...] = jnp.zeros(acc_ref.shape, F32)
        l_ref[...] = jnp.zeros(l_ref.shape, F32)
        m_ref[...] = jnp.full(m_ref.shape, NEG, F32)

        def kv_tile(j, diag=None):
            q0 = 0 if diag is None else diag * tk
            c = tile_offset(j)
            for bb in range(nb):
                v_t = vt_ref[bb, 0, j]
                for sub in range(2):
                    s = lax.dot_general(kp_ref[bb, sub, j], qp_ref[bb, sub, q0:, :], NT_DIMS,
                                        preferred_element_type=F32)
                    if diag is not None:
                        s = jnp.where(causal_mask(tq - q0), s, NEG)
                    m_old = m_ref[bb, sub, :, q0:]
                    m_new = jnp.maximum(m_old, jnp.max(s, axis=0, keepdims=True) + c)
                    alpha = jnp.exp2(m_old - m_new)
                    p = jnp.exp2(s - (m_new - c))
                    pv = jnp.dot(v_t, p.astype(BF16), preferred_element_type=F32)
                    acc_ref[bb, sub, :, q0:] = acc_ref[bb, sub, :, q0:] * alpha + pv
                    l_ref[bb, sub, :, q0:] = (l_ref[bb, sub, :, q0:] * alpha
                                              + jnp.sum(p, axis=0, keepdims=True))
                    m_ref[bb, sub, :, q0:] = m_new

        def body(j, carry):
            kv_tile(j)
            return carry

        lax.fori_loop(0, i * n_diag, body, 0)
        for r in range(n_diag):
            kv_tile(i * n_diag + r, diag=r)

    lam = (jnp.exp(jnp.sum(lq1_ref[...] * lk1_ref[...], axis=-1, keepdims=True))
           - jnp.exp(jnp.sum(lq2_ref[...] * lk2_ref[...], axis=-1, keepdims=True))
           + lambda_init)
    for bb in range(nb):
        o_t = (acc_ref[bb, 0] * (1.0 / l_ref[bb, 0])
               - lam * (acc_ref[bb, 1] * (1.0 / l_ref[bb, 1])))
        ms = jnp.mean(o_t * o_t, axis=0, keepdims=True)
        o = (o_t * lax.rsqrt(ms + EPS)).T
        o = o * gs_ref[...] * (1.0 - lambda_init)
        og_ref[bb] = (o * _silu(z_ref[bb].astype(F32))).astype(BF16)


def _attn_a(bound, qp, kp, vt, z, lq1, lk1, lq2, lk2, gs, lambda_init):
    bsz, n_sub, seq, _ = qp.shape
    nk = seq // TK
    nq = seq // TQ
    d = z.shape[-1]
    vd = 2 * A_HEAD_DIM
    kp5 = kp.reshape(bsz, n_sub, nk, TK, LANES)
    nb = A_BATCH_ROWS if bsz % A_BATCH_ROWS == 0 else 1
    kern = functools.partial(_attn_a_kernel, slopes2=_alibi_slopes_log2(A_HEADS), lambda_init=lambda_init)
    vec = lambda n: pl.BlockSpec((1, n), lambda b, h, i: (0, 0))
    return pl.pallas_call(
        kern,
        grid=(bsz // nb, A_HEADS, nq),
        in_specs=[pl.BlockSpec(memory_space=pltpu.SMEM),
                  pl.BlockSpec((nb, 2, TQ, LANES), lambda b, h, i: (b, h, i, 0)),
                  pl.BlockSpec((nb, 2, nk, TK, LANES), lambda b, h, i: (b, h, 0, 0, 0)),
                  pl.BlockSpec((nb, 1, nk, V_ROWS, TK), lambda b, h, i: (b, h, 0, 0, 0)),
                  pl.BlockSpec((nb, TQ, vd), lambda b, h, i: (b, i, h)),
                  vec(A_HEAD_DIM), vec(A_HEAD_DIM), vec(A_HEAD_DIM), vec(A_HEAD_DIM), vec(vd)],
        out_specs=pl.BlockSpec((nb, TQ, vd), lambda b, h, i: (b, i, h)),
        out_shape=jax.ShapeDtypeStruct((bsz, seq, d), BF16),
        scratch_shapes=[pltpu.VMEM((nb, 2, V_ROWS, TQ), F32), pltpu.VMEM((nb, 2, 1, TQ), F32),
                        pltpu.VMEM((nb, 2, 1, TQ), F32), pltpu.VMEM((nb, 2, TK, TQ), BF16)],
        compiler_params=pltpu.CompilerParams(dimension_semantics=("arbitrary",) * 3,
                                             vmem_limit_bytes=VMEM_LIMIT),
        name="attn_a",
    )(bound, qp, kp5, vt, z, lq1, lk1, lq2, lk2, gs)


def _mid_kernel(x_ref, og_ref, mod_ref, wo_ref, gkv_ref, gb_ref, wk_ref, wvt_ref, wqb_ref, wzb_ref,
                gks_ref, gqb_ref, h1_ref, qb_ref, ks_ref, vst_ref, zb_ref):
    tm = x_ref.shape[1]
    first_half = lax.broadcasted_iota(jnp.int32, (tm, LANES), 1) < B_HEAD_DIM
    gqb = gqb_ref[...]
    for bb in range(x_ref.shape[0]):
        gate = mod_ref[bb, 2:3, :]
        h1 = x_ref[bb] + gate * jnp.dot(og_ref[bb], wo_ref[...], preferred_element_type=F32)
        h1_ref[bb] = h1

        ms = jnp.mean(h1 * h1, axis=-1, keepdims=True)
        r = h1 * lax.rsqrt(ms + EPS)
        ukv = ((r * gkv_ref[...]) * (1.0 + mod_ref[bb, 4:5, :]) + mod_ref[bb, 3:4, :]).astype(BF16)
        ub = ((r * gb_ref[...]) * (1.0 + mod_ref[bb, 6:7, :]) + mod_ref[bb, 5:6, :]).astype(BF16)

        kk = jnp.dot(ukv, wk_ref[...], preferred_element_type=F32)
        ks_ref[bb] = _pair_norm(kk, gks_ref[...], first_half).astype(BF16)
        v_t = lax.dot_general(wvt_ref[...], ukv, NT_DIMS, preferred_element_type=F32)
        for g in range(B_KV_HEADS):
            vst_ref[bb, g] = v_t[g * B_HEAD_DIM:(g + 1) * B_HEAD_DIM].astype(BF16)

        for pp in range(B_HEADS // 4):
            qq = jnp.dot(ub, wqb_ref[:, pp * 2 * LANES:(pp + 1) * 2 * LANES], preferred_element_type=F32)
            for hl in range(2):
                h0 = 4 * pp + 2 * hl
                qn = _pair_norm(qq[:, hl * LANES:(hl + 1) * LANES], gqb, first_half)
                swapped = pltpu.roll(qn, LANES // 2, axis=1)
                if h0 < B_GROUP:
                    qb_ref[bb, h0] = jnp.where(first_half, qn, 0.0).astype(BF16)
                    qb_ref[bb, h0 + 1] = jnp.where(first_half, swapped, 0.0).astype(BF16)
                else:
                    qb_ref[bb, h0] = jnp.where(first_half, 0.0, swapped).astype(BF16)
                    qb_ref[bb, h0 + 1] = jnp.where(first_half, 0.0, qn).astype(BF16)
        zb_ref[bb] = jnp.dot(ub, wzb_ref[...], preferred_element_type=F32).astype(BF16)


def _mid(x, og, mod, wo, gkv, gb, wk, wvt, wqb, wzb, gks, gqb):
    bsz, seq, d = x.shape
    nt = seq // TM
    nr = _rows_per_step(bsz)
    tok = lambda: pl.BlockSpec((nr, TM, d), lambda b, t: (b, t, 0))
    return pl.pallas_call(
        _mid_kernel,
        grid=(bsz // nr, nt),
        in_specs=[tok(), tok(),
                  pl.BlockSpec((nr, 8, d), lambda b, t: (b, 0, 0)),
                  _const_spec(wo.shape), _const_spec((1, d)), _const_spec((1, d)),
                  _const_spec(wk.shape), _const_spec(wvt.shape), _const_spec(wqb.shape),
                  _const_spec(wzb.shape), _const_spec((1, LANES)), _const_spec((1, LANES))],
        out_specs=[tok(),
                   pl.BlockSpec((nr, B_HEADS, TM, LANES), lambda b, t: (b, 0, t, 0)),
                   pl.BlockSpec((nr, TM, LANES), lambda b, t: (b, t, 0)),
                   pl.BlockSpec((nr, B_KV_HEADS, B_HEAD_DIM, TM), lambda b, t: (b, 0, 0, t)),
                   tok()],
        out_shape=[jax.ShapeDtypeStruct((bsz, seq, d), F32),
                   jax.ShapeDtypeStruct((bsz, B_HEADS, seq, LANES), BF16),
                   jax.ShapeDtypeStruct((bsz, seq, LANES), BF16),
                   jax.ShapeDtypeStruct((bsz, B_KV_HEADS, B_HEAD_DIM, seq), BF16),
                   jax.ShapeDtypeStruct((bsz, seq, d), BF16)],
        compiler_params=_params(),
        name="mid_proj",
    )(x, og, mod, wo, gkv, gb, wk, wvt, wqb, wzb, gks, gqb)


def _final_kernel(bound_ref, h1_ref, qb_ref, ksc_ref, ksp_ref, vtc_ref, vtp_ref, zb_ref, wo_ref, mod_ref,
                  bias_ref, sink_ref, out_ref, o_scr):
    t = pl.program_id(1)
    tm = h1_ref.shape[1]
    w = WINDOW
    first_tile = t == 0
    prev_rows = lax.broadcasted_iota(jnp.int32, (2 * w, B_GROUP * w), 0) < w
    bound = bound_ref[0]

    def attend(bounded):
        if bounded:
            ref = [jnp.maximum(sink_ref[g], bound) for g in range(B_KV_HEADS)]
            ref_terms = ([bias_ref[g] - ref[g] for g in range(B_KV_HEADS)],
                         [jnp.exp2(sink_ref[g] - ref[g]) for g in range(B_KV_HEADS)])
        for bb in range(h1_ref.shape[0]):
            attend_row(bb, bounded, ref_terms if bounded else None)

    def attend_row(bb, bounded, ref_terms):
        if bounded:
            bias_rel, sink_term = ref_terms
        for nb in range(tm // w):
            if nb == 0:
                kband = jnp.concatenate([ksp_ref[bb], ksc_ref[bb, 0:w, :]], axis=0)
            else:
                kband = ksc_ref[bb, (nb - 1) * w:(nb + 1) * w, :]
            for g in range(B_KV_HEADS):
                if nb == 0:
                    vband = jnp.concatenate([vtp_ref[bb, g], vtc_ref[bb, g, :, 0:w]], axis=1)
                else:
                    vband = vtc_ref[bb, g, :, (nb - 1) * w:(nb + 1) * w]
                qs = jnp.concatenate([qb_ref[bb, B_GROUP * g + hh, nb * w:(nb + 1) * w, :]
                                      for hh in range(B_GROUP)], axis=0)
                s = lax.dot_general(kband, qs, NT_DIMS, preferred_element_type=F32)
                before_start = first_tile & prev_rows
                if bounded:
                    e = jnp.exp2(s + bias_rel[g])
                    if nb == 0:
                        e = jnp.where(before_start, 0.0, e)
                    den = jnp.sum(e, axis=0, keepdims=True) + sink_term[g]
                else:
                    s = s + bias_ref[g]
                    if nb == 0:
                        s = jnp.where(before_start, NEG, s)
                    sink = sink_ref[g]
                    m = jnp.maximum(jnp.max(s, axis=0, keepdims=True), sink)
                    e = jnp.exp2(s - m)
                    den = jnp.sum(e, axis=0, keepdims=True) + jnp.exp2(sink - m)
                o_t = jnp.dot(vband, e.astype(BF16), preferred_element_type=F32) * (1.0 / den)
                for pr in range(B_GROUP // 2):
                    pair = jnp.concatenate([o_t[:, (2 * pr) * w:(2 * pr + 1) * w],
                                            o_t[:, (2 * pr + 1) * w:(2 * pr + 2) * w]], axis=0)
                    col = (B_GROUP * g + 2 * pr) * B_HEAD_DIM
                    o_scr[bb, nb * w:(nb + 1) * w, col:col + 2 * B_HEAD_DIM] = pair.T

        og = (o_scr[bb] * _silu(zb_ref[bb].astype(F32))).astype(BF16)
        out_ref[bb] = h1_ref[bb] + mod_ref[bb, 7:8, :] * jnp.dot(og, wo_ref[...], preferred_element_type=F32)

    @pl.when(bound <= BOUND_MAX)
    def _():
        attend(True)

    @pl.when(bound > BOUND_MAX)
    def _():
        attend(False)


def _final(bound, h1, qb, ks, vst, zb, wo, mod, bias, sink):
    bsz, seq, d = h1.shape
    nt = seq // TM
    nblk = TM // WINDOW
    nr = _rows_per_step(bsz)
    tok = lambda: pl.BlockSpec((nr, TM, d), lambda b, t: (b, t, 0))
    prev = lambda t: jnp.maximum(t * nblk - 1, 0)
    return pl.pallas_call(
        _final_kernel,
        grid=(bsz // nr, nt),
        in_specs=[pl.BlockSpec(memory_space=pltpu.SMEM),
                  tok(),
                  pl.BlockSpec((nr, B_HEADS, TM, LANES), lambda b, t: (b, 0, t, 0)),
                  pl.BlockSpec((nr, TM, LANES), lambda b, t: (b, t, 0)),
                  pl.BlockSpec((nr, WINDOW, LANES), lambda b, t: (b, prev(t), 0)),
                  pl.BlockSpec((nr, B_KV_HEADS, B_HEAD_DIM, TM), lambda b, t: (b, 0, 0, t)),
                  pl.BlockSpec((nr, B_KV_HEADS, B_HEAD_DIM, WINDOW), lambda b, t: (b, 0, 0, prev(t))),
                  tok(),
                  _const_spec(wo.shape),
                  pl.BlockSpec((nr, 8, d), lambda b, t: (b, 0, 0)),
                  _const_spec(bias.shape), _const_spec(sink.shape)],
        out_specs=tok(),
        out_shape=jax.ShapeDtypeStruct((bsz, seq, d), F32),
        scratch_shapes=[pltpu.VMEM((nr, TM, d), F32)],
        compiler_params=_params(),
        name="attn_b_out",
    )(bound, h1, qb, ks, ks, vst, vst, zb, wo, mod, bias, sink)


def _logit_bound(gq, gk, head_dim):
    return (head_dim * BOUND_SLACK * jnp.max(jnp.abs(gq)) * jnp.max(jnp.abs(gk))).reshape(1)


def _pair_gain(g, scale):
    return jnp.tile(g.astype(F32) * scale, 2).reshape(1, LANES)


def _swa_bias_table():
    slopes = np.asarray(_alibi_slopes_log2(B_HEADS), dtype=np.float32)
    k = np.arange(2 * WINDOW)[:, None]
    q = np.arange(WINDOW)[None, :]
    dist = q + WINDOW - k
    valid = (dist >= 0) & (dist < WINDOW)
    tab = np.empty((B_KV_HEADS, 2 * WINDOW, B_GROUP * WINDOW), np.float32)
    for g in range(B_KV_HEADS):
        for hh in range(B_GROUP):
            blk = np.where(valid, -slopes[B_GROUP * g + hh] * dist.astype(np.float32), np.float32(NEG))
            tab[g, :, hh * WINDOW:(hh + 1) * WINDOW] = blk
    return jnp.asarray(tab)


def kernel(x, c, a_norm_g, a_ada_w, a_ada_b, a_w_in, a_q_norm_g, a_k_norm_g, a_lambda_q1, a_lambda_k1,
           a_lambda_q2, a_lambda_k2, a_subln_g, a_w_out, kv_norm_g, kv_ada_w, kv_ada_b, w_kv, kv_k_norm_g,
           b_norm_g, b_ada_w, b_ada_b, b_w_in, b_q_norm_g, b_sinks, b_w_out):
    bsz, seq, d = x.shape
    assert a_norm_g.shape[0] == 1 and b_norm_g.shape[0] == 1, "one layer of each mixer"
    assert TM == TK and seq % TM == 0 and seq % TQ == 0 and TQ % TK == 0 and TM % WINDOW == 0
    a_inner = A_HEADS * 2 * A_HEAD_DIM
    b_inner = B_HEADS * B_HEAD_DIM

    mod = _modulation(c, a_ada_w[0], a_ada_b[0], kv_ada_w, kv_ada_b, b_ada_w[0], b_ada_b[0])

    w_in = a_w_in[0]
    wqk = w_in[:, 0:2 * a_inner].astype(BF16)
    wvt = w_in[:, 2 * a_inner:3 * a_inner].T.astype(BF16)
    wz = w_in[:, 3 * a_inner:].astype(BF16)
    gq = _pair_gain(a_q_norm_g[0], A_HEAD_DIM ** -0.5 * LOG2E)
    gk = _pair_gain(a_k_norm_g[0], 1.0)
    bound_a = _logit_bound(gq, gk, A_HEAD_DIM)

    qp, kp, vt, z = _proj_a(x, mod, a_norm_g[0].reshape(1, d), wqk, wz, wvt, gq, gk)

    lambda_init = 0.8 - 0.6 * math.exp(-0.3 * 0)
    row = lambda v: v.astype(F32).reshape(1, -1)
    og = _attn_a(bound_a, qp, kp, vt, z, row(a_lambda_q1[0]), row(a_lambda_k1[0]), row(a_lambda_q2[0]),
                 row(a_lambda_k2[0]), row(a_subln_g[0]), lambda_init)

    kv_w = B_KV_HEADS * B_HEAD_DIM
    wks = w_kv[:, 0:kv_w].astype(BF16)
    wvst = w_kv[:, kv_w:].T.astype(BF16)
    wqb = b_w_in[0][:, 0:b_inner].astype(BF16)
    wzb = b_w_in[0][:, b_inner:].astype(BF16)
    gks = _pair_gain(kv_k_norm_g, 1.0)
    gqb = _pair_gain(b_q_norm_g[0], B_HEAD_DIM ** -0.5 * LOG2E)

    h1, qb, ks, vst, zb = _mid(x, og, mod, a_w_out[0].astype(BF16), kv_norm_g.reshape(1, d),
                               b_norm_g[0].reshape(1, d), wks, wvst, wqb, wzb, gks, gqb)

    sink = jnp.repeat(b_sinks[0].astype(F32).reshape(B_KV_HEADS, B_GROUP) * LOG2E, WINDOW, axis=1)
    sink = sink.reshape(B_KV_HEADS, 1, B_GROUP * WINDOW)
    bound_b = _logit_bound(gqb, gks, B_HEAD_DIM)
    return _final(bound_b, h1, qb, ks, vst, zb, b_w_out[0].astype(BF16), mod, _swa_bias_table(), sink)
```

```python
import functools
import math

import numpy as np
import jax
import jax.numpy as jnp
from jax import lax
from jax.experimental import pallas as pl
from jax.experimental.pallas import tpu as pltpu

EPS = 1e-6
LANES = 128
A_HEADS = 8
A_HEAD_DIM = 64
B_HEADS = 16
B_KV_HEADS = 2
B_GROUP = B_HEADS // B_KV_HEADS
B_HEAD_DIM = 64
WINDOW = 128
NEG = -1e30
V_ROWS = 2 * A_HEAD_DIM

TM = 512
TQ = 1024
TK = 512
A_BATCH_ROWS = 2
VMEM_LIMIT = 56 * 1024 * 1024

LOG2E = 1.4426950408889634
EXP2_ZERO = 152.0
BOUND_MAX = 50.0
BOUND_SLACK = 1.01

F32 = jnp.float32
BF16 = jnp.bfloat16
NT_DIMS = (((1,), (1,)), ((), ()))


def _alibi_slopes(n_heads):
    return [2.0 ** (-8.0 * (h + 1) / n_heads) for h in range(n_heads)]


def _alibi_slopes_log2(n_heads):
    return [s * LOG2E for s in _alibi_slopes(n_heads)]


def _silu(x):
    return x * (1.0 / (1.0 + jnp.exp(-x)))


def _const_spec(shape):
    zeros = (0,) * len(shape)
    return pl.BlockSpec(shape, lambda *_: zeros, pipeline_mode=pl.Buffered(1))


def _rows_per_step(bsz):
    return A_BATCH_ROWS if bsz % A_BATCH_ROWS == 0 else 1


def _params():
    return pltpu.CompilerParams(dimension_semantics=("arbitrary",) * 2, vmem_limit_bytes=VMEM_LIMIT)


def _mod_kernel(c_ref, aw_ref, kw_ref, bw_ref, ab_ref, kb_ref, bb_ref, o_ref):
    j = pl.program_id(0)
    sc = _silu(c_ref[...])

    def emit(w_ref, b_ref):
        o_ref[...] = jnp.dot(sc, w_ref[...], precision=lax.Precision.HIGHEST,
                             preferred_element_type=F32) + b_ref[...]

    @pl.when(j < 3)
    def _():
        emit(aw_ref, ab_ref)

    @pl.when((j >= 3) & (j < 5))
    def _():
        emit(kw_ref, kb_ref)

    @pl.when(j >= 5)
    def _():
        emit(bw_ref, bb_ref)


def _modulation(c, a_w, a_b, kv_w, kv_b, b_w, b_b):
    bsz, d = c.shape
    wspec = lambda f: pl.BlockSpec((d, d), lambda j: (0, f(j)))
    bspec = lambda f: pl.BlockSpec((1, d), lambda j: (0, f(j)))
    fa = lambda j: jnp.minimum(j, 2)
    fk = lambda j: jnp.clip(j - 3, 0, 1)
    fb = lambda j: jnp.clip(j - 5, 0, 2)
    out = pl.pallas_call(
        _mod_kernel,
        grid=(8,),
        in_specs=[pl.BlockSpec((bsz, d), lambda j: (0, 0)),
                  wspec(fa), wspec(fk), wspec(fb), bspec(fa), bspec(fk), bspec(fb)],
        out_specs=pl.BlockSpec((bsz, d), lambda j: (0, j)),
        out_shape=jax.ShapeDtypeStruct((bsz, 8 * d), F32),
        compiler_params=pltpu.CompilerParams(dimension_semantics=("arbitrary",),
                                             vmem_limit_bytes=VMEM_LIMIT),
        name="ada_modulation",
    )(c, a_w, kv_w, b_w, a_b.reshape(1, -1), kv_b.reshape(1, -1), b_b.reshape(1, -1))
    return out.reshape(bsz, 8, d)


def _modulated_norm(x, g, scale, shift):
    ms = jnp.mean(x * x, axis=-1, keepdims=True)
    return (x * lax.rsqrt(ms + EPS) * g) * (1.0 + scale) + shift


def _pair_norm(x, gain2, first_half):
    y = x * x
    ss_a = jnp.sum(jnp.where(first_half, y, 0.0), axis=-1, keepdims=True)
    ss_b = jnp.sum(jnp.where(first_half, 0.0, y), axis=-1, keepdims=True)
    ms = jnp.where(first_half, ss_a, ss_b) * (2.0 / LANES)
    return x * lax.rsqrt(ms + EPS) * gain2


def _split_pair(xn, first_half, fill_a, fill_b):
    return (jnp.where(first_half, xn, fill_a),
            jnp.where(first_half, pltpu.roll(xn, LANES // 2, axis=1), fill_b))


def _proj_a_kernel(x_ref, mod_ref, g_ref, wqk_ref, wz_ref, wvt_ref, gq_ref, gk_ref,
                   qp_ref, kp_ref, vt_ref, z_ref, *, slopes):
    tm = x_ref.shape[1]
    lane = lax.broadcasted_iota(jnp.int32, (tm, LANES), 1)
    first_half = lane < A_HEAD_DIM
    col_hi = lane == A_HEAD_DIM
    col_lo = lane == A_HEAD_DIM + 1
    pos = lax.broadcasted_iota(jnp.int32, (tm, LANES), 0).astype(F32)
    q_fill = jnp.where(col_hi | col_lo, 1.0, 0.0)
    gq = gq_ref[...]
    gk = gk_ref[...]
    a_inner = A_HEADS * 2 * A_HEAD_DIM
    for bb in range(x_ref.shape[0]):
        u = _modulated_norm(x_ref[bb], g_ref[...], mod_ref[bb, 1:2, :], mod_ref[bb, 0:1, :]).astype(BF16)
        for hp in range(A_HEADS // 2):
            c0 = hp * 2 * LANES
            qq = jnp.dot(u, wqk_ref[:, c0:c0 + 2 * LANES], preferred_element_type=F32)
            kk = jnp.dot(u, wqk_ref[:, a_inner + c0:a_inner + c0 + 2 * LANES],
                         preferred_element_type=F32)
            for hl in range(2):
                h = 2 * hp + hl
                bias = pos * slopes[h]
                bias_hi = bias.astype(BF16).astype(F32)
                k_fill = jnp.where(col_hi, bias_hi, jnp.where(col_lo, bias - bias_hi, 0.0))
                qn = _pair_norm(qq[:, hl * LANES:(hl + 1) * LANES], gq, first_half)
                kn = _pair_norm(kk[:, hl * LANES:(hl + 1) * LANES], gk, first_half)
                for s, (qs, ks) in enumerate(zip(_split_pair(qn, first_half, q_fill, q_fill),
                                                 _split_pair(kn, first_half, k_fill, k_fill))):
                    qp_ref[bb, 2 * h + s] = qs.astype(BF16)
                    kp_ref[bb, 2 * h + s] = ks.astype(BF16)

        z_ref[bb] = jnp.dot(u, wz_ref[...], preferred_element_type=F32).astype(BF16)

        v_t = lax.dot_general(wvt_ref[...], u, NT_DIMS, preferred_element_type=F32)
        for h in range(A_HEADS):
            vt_ref[bb, h, 0] = v_t[h * V_ROWS:(h + 1) * V_ROWS].astype(BF16)


def _proj_a(x, mod, g, wqk, wz, wvt, gq, gk):
    bsz, seq, d = x.shape
    nt = seq // TM
    n_sub = 2 * A_HEADS
    kern = functools.partial(_proj_a_kernel, slopes=_alibi_slopes_log2(A_HEADS))
    nr = _rows_per_step(bsz)
    return pl.pallas_call(
        kern,
        grid=(bsz // nr, nt),
        in_specs=[pl.BlockSpec((nr, TM, d), lambda b, t: (b, t, 0)),
                  pl.BlockSpec((nr, 8, d), lambda b, t: (b, 0, 0)),
                  _const_spec((1, d)),
                  _const_spec(wqk.shape), _const_spec(wz.shape), _const_spec(wvt.shape),
                  _const_spec((1, LANES)), _const_spec((1, LANES))],
        out_specs=[pl.BlockSpec((nr, n_sub, TM, LANES), lambda b, t: (b, 0, t, 0)),
                   pl.BlockSpec((nr, n_sub, TM, LANES), lambda b, t: (b, 0, t, 0)),
                   pl.BlockSpec((nr, A_HEADS, 1, V_ROWS, TM), lambda b, t: (b, 0, t, 0, 0)),
                   pl.BlockSpec((nr, TM, d), lambda b, t: (b, t, 0))],
        out_shape=[jax.ShapeDtypeStruct((bsz, n_sub, seq, LANES), BF16),
                   jax.ShapeDtypeStruct((bsz, n_sub, seq, LANES), BF16),
                   jax.ShapeDtypeStruct((bsz, A_HEADS, nt, V_ROWS, TM), BF16),
                   jax.ShapeDtypeStruct((bsz, seq, d), BF16)],
        compiler_params=_params(),
        name="proj_a",
    )(x, mod, g, wqk, wz, wvt, gq, gk)


def _attn_a_kernel(bound_ref, qp_ref, kp_ref, vt_ref, z_ref, lq1_ref, lk1_ref, lq2_ref, lk2_ref, gs_ref,
                   og_ref, acc_ref, l_ref, m_ref, p_ref, *, slopes2, lambda_init):
    h = pl.program_id(1)
    i = pl.program_id(2)
    nb = qp_ref.shape[0]
    tq = qp_ref.shape[2]
    tk = kp_ref.shape[3]
    vd = 2 * A_HEAD_DIM

    slope2 = jnp.float32(slopes2[0])
    zero_dist = jnp.int32(math.ceil(EXP2_ZERO / slopes2[0]))
    for hh in range(1, A_HEADS):
        slope2 = jnp.where(h == hh, jnp.float32(slopes2[hh]), slope2)
        zero_dist = jnp.where(h == hh, jnp.int32(math.ceil(EXP2_ZERO / slopes2[hh])), zero_dist)

    bound = bound_ref[0]

    def tile_offset(j):
        return slope2 * (j * tk - i * tq).astype(F32)

    n_diag = tq // tk

    def causal_mask(width):
        krow = lax.broadcasted_iota(jnp.int32, (tk, width), 0)
        qcol = lax.broadcasted_iota(jnp.int32, (tk, width), 1)
        return krow <= qcol

    @pl.when(bound <= BOUND_MAX)
    def _():
        q_local = lax.broadcasted_iota(jnp.int32, (1, tq), 1).astype(F32)
        ref = bound + slope2 * q_local

        def probs(j, bb, sub, q0):
            s = lax.dot_general(kp_ref[bb, sub, j], qp_ref[bb, sub, q0:, :], NT_DIMS,
                                preferred_element_type=F32)
            return jnp.exp2(s + (tile_offset(j) - ref[:, q0:]))

        def stage_a(j):
            for bb in range(nb):
                for sub in range(2):
                    e = probs(j, bb, sub, 0)
                    l_ref[bb, sub] += jnp.sum(e, axis=0, keepdims=True)
                    p_ref[bb, sub] = e.astype(BF16)

        def stage_b(j):
            for bb in range(nb):
                v_t = vt_ref[bb, 0, j]
                for sub in range(2):
                    acc_ref[bb, sub] += jnp.dot(v_t, p_ref[bb, sub], preferred_element_type=F32)

        def diag_tile(r):
            q0 = r * tk
            j = i * n_diag + r
            for bb in range(nb):
                v_t = vt_ref[bb, 0, j]
                for sub in range(2):
                    e = jnp.where(causal_mask(tq - q0), probs(j, bb, sub, q0), 0.0)
                    l_ref[bb, sub, :, q0:] += jnp.sum(e, axis=0, keepdims=True)
                    acc_ref[bb, sub, :, q0:] += jnp.dot(v_t, e.astype(BF16), preferred_element_type=F32)

        first = jnp.maximum(i * tq - zero_dist + 1, 0) // tk
        last = i * n_diag

        def zero_accumulators():
            acc_ref[...] = jnp.zeros(acc_ref.shape, F32)
            l_ref[...] = jnp.zeros(l_ref.shape, F32)

        n_off = last - first
        odd = (n_off >= 2) & (n_off % 2 == 0)

        @pl.when((n_off >= 1) & jnp.logical_not(odd))
        def _():
            zero_accumulators()
            stage_a(first)

        @pl.when(odd)
        def _():
            zero_accumulators()
            stage_a(first)
            stage_b(first)
            stage_a(first + 1)

        @pl.when(n_off <= 0)
        def _():
            zero_accumulators()
            p_ref[...] = jnp.zeros(p_ref.shape, BF16)

        start = first + 1 + odd.astype(jnp.int32)

        def body(t, carry):
            j = start + 2 * t
            stage_b(j - 1)
            stage_a(j)
            stage_b(j)
            stage_a(j + 1)
            return carry

        lax.fori_loop(0, jnp.maximum(last - start, 0) // 2, body, 0)
        stage_b(jnp.maximum(last - 1, 0))
        for r in range(n_diag):
            diag_tile(r)

    @pl.when(bound > BOUND_MAX)
    def _():
        acc_ref[...] = jnp.zeros(acc_ref.shape, F32)
        l_ref[...] = jnp.zeros(l_ref.shape, F32)
        m_ref[...] = jnp.full(m_ref.shape, NEG, F32)

        def kv_tile(j, diag=None):
            q0 = 0 if diag is None else diag * tk
            c = tile_offset(j)
            for bb in range(nb):
                v_t = vt_ref[bb, 0, j]
                for sub in range(2):
                    s = lax.dot_general(kp_ref[bb, sub, j], qp_ref[bb, sub, q0:, :], NT_DIMS,
                                        preferred_element_type=F32)
                    if diag is not None:
                        s = jnp.where(causal_mask(tq - q0), s, NEG)
                    m_old = m_ref[bb, sub, :, q0:]
                    m_new = jnp.maximum(m_old, jnp.max(s, axis=0, keepdims=True) + c)
                    alpha = jnp.exp2(m_old - m_new)
                    p = jnp.exp2(s - (m_new - c))
                    pv = jnp.dot(v_t, p.astype(BF16), preferred_element_type=F32)
                    acc_ref[bb, sub, :, q0:] = acc_ref[bb, sub, :, q0:] * alpha + pv
                    l_ref[bb, sub, :, q0:] = (l_ref[bb, sub, :, q0:] * alpha
                                              + jnp.sum(p, axis=0, keepdims=True))
                    m_ref[bb, sub, :, q0:] = m_new

        def body(j, carry):
            kv_tile(j)
            return carry

        lax.fori_loop(0, i * n_diag, body, 0)
        for r in range(n_diag):
            kv_tile(i * n_diag + r, diag=r)

    lam = (jnp.exp(jnp.sum(lq1_ref[...] * lk1_ref[...], axis=-1, keepdims=True))
           - jnp.exp(jnp.sum(lq2_ref[...] * lk2_ref[...], axis=-1, keepdims=True))
           + lambda_init)
    for bb in range(nb):
        o_t = (acc_ref[bb, 0] * (1.0 / l_ref[bb, 0])
               - lam * (acc_ref[bb, 1] * (1.0 / l_ref[bb, 1])))
        ms = jnp.mean(o_t * o_t, axis=0, keepdims=True)
        o = (o_t * lax.rsqrt(ms + EPS)).T
        o = o * gs_ref[...] * (1.0 - lambda_init)
        og_ref[bb] = (o * _silu(z_ref[bb].astype(F32))).astype(BF16)


def _attn_a(bound, qp, kp, vt, z, lq1, lk1, lq2, lk2, gs, lambda_init):
    bsz, n_sub, seq, _ = qp.shape
    nk = seq // TK
    nq = seq // TQ
    d = z.shape[-1]
    vd = 2 * A_HEAD_DIM
    kp5 = kp.reshape(bsz, n_sub, nk, TK, LANES)
    nb = A_BATCH_ROWS if bsz % A_BATCH_ROWS == 0 else 1
    kern = functools.partial(_attn_a_kernel, slopes2=_alibi_slopes_log2(A_HEADS), lambda_init=lambda_init)
    vec = lambda n: pl.BlockSpec((1, n), lambda b, h, i: (0, 0))
    return pl.pallas_call(
        kern,
        grid=(bsz // nb, A_HEADS, nq),
        in_specs=[pl.BlockSpec(memory_space=pltpu.SMEM),
                  pl.BlockSpec((nb, 2, TQ, LANES), lambda b, h, i: (b, h, i, 0)),
                  pl.BlockSpec((nb, 2, nk, TK, LANES), lambda b, h, i: (b, h, 0, 0, 0)),
                  pl.BlockSpec((nb, 1, nk, V_ROWS, TK), lambda b, h, i: (b, h, 0, 0, 0)),
                  pl.BlockSpec((nb, TQ, vd), lambda b, h, i: (b, i, h)),
                  vec(A_HEAD_DIM), vec(A_HEAD_DIM), vec(A_HEAD_DIM), vec(A_HEAD_DIM), vec(vd)],
        out_specs=pl.BlockSpec((nb, TQ, vd), lambda b, h, i: (b, i, h)),
        out_shape=jax.ShapeDtypeStruct((bsz, seq, d), BF16),
        scratch_shapes=[pltpu.VMEM((nb, 2, V_ROWS, TQ), F32), pltpu.VMEM((nb, 2, 1, TQ), F32),
                        pltpu.VMEM((nb, 2, 1, TQ), F32), pltpu.VMEM((nb, 2, TK, TQ), BF16)],
        compiler_params=pltpu.CompilerParams(dimension_semantics=("arbitrary",) * 3,
                                             vmem_limit_bytes=VMEM_LIMIT),
        name="attn_a",
    )(bound, qp, kp5, vt, z, lq1, lk1, lq2, lk2, gs)


def _mid_kernel(x_ref, og_ref, mod_ref, wo_ref, gkv_ref, gb_ref, wk_ref, wvt_ref, wqb_ref, wzb_ref,
                gks_ref, gqb_ref, h1_ref, qb_ref, ks_ref, vst_ref, zb_ref):
    tm = x_ref.shape[1]
    first_half = lax.broadcasted_iota(jnp.int32, (tm, LANES), 1) < B_HEAD_DIM
    gqb = gqb_ref[...]
    for bb in range(x_ref.shape[0]):
        gate = mod_ref[bb, 2:3, :]
        h1 = x_ref[bb] + gate * jnp.dot(og_ref[bb], wo_ref[...], preferred_element_type=F32)
        h1_ref[bb] = h1

        ms = jnp.mean(h1 * h1, axis=-1, keepdims=True)
        r = h1 * lax.rsqrt(ms + EPS)
        ukv = ((r * gkv_ref[...]) * (1.0 + mod_ref[bb, 4:5, :]) + mod_ref[bb, 3:4, :]).astype(BF16)
        ub = ((r * gb_ref[...]) * (1.0 + mod_ref[bb, 6:7, :]) + mod_ref[bb, 5:6, :]).astype(BF16)

        kk = jnp.dot(ukv, wk_ref[...], preferred_element_type=F32)
        ks_ref[bb] = _pair_norm(kk, gks_ref[...], first_half).astype(BF16)
        v_t = lax.dot_general(wvt_ref[...], ukv, NT_DIMS, preferred_element_type=F32)
        for g in range(B_KV_HEADS):
            vst_ref[bb, g] = v_t[g * B_HEAD_DIM:(g + 1) * B_HEAD_DIM].astype(BF16)

        for pp in range(B_HEADS // 4):
            qq = jnp.dot(ub, wqb_ref[:, pp * 2 * LANES:(pp + 1) * 2 * LANES], preferred_element_type=F32)
            for hl in range(2):
                h0 = 4 * pp + 2 * hl
                qn = _pair_norm(qq[:, hl * LANES:(hl + 1) * LANES], gqb, first_half)
                swapped = pltpu.roll(qn, LANES // 2, axis=1)
                if h0 < B_GROUP:
                    qb_ref[bb, h0] = jnp.where(first_half, qn, 0.0).astype(BF16)
                    qb_ref[bb, h0 + 1] = jnp.where(first_half, swapped, 0.0).astype(BF16)
                else:
                    qb_ref[bb, h0] = jnp.where(first_half, 0.0, swapped).astype(BF16)
                    qb_ref[bb, h0 + 1] = jnp.where(first_half, 0.0, qn).astype(BF16)
        zb_ref[bb] = jnp.dot(ub, wzb_ref[...], preferred_element_type=F32).astype(BF16)


def _mid(x, og, mod, wo, gkv, gb, wk, wvt, wqb, wzb, gks, gqb):
    bsz, seq, d = x.shape
    nt = seq // TM
    nr = _rows_per_step(bsz)
    tok = lambda: pl.BlockSpec((nr, TM, d), lambda b, t: (b, t, 0))
    return pl.pallas_call(
        _mid_kernel,
        grid=(bsz // nr, nt),
        in_specs=[tok(), tok(),
                  pl.BlockSpec((nr, 8, d), lambda b, t: (b, 0, 0)),
                  _const_spec(wo.shape), _const_spec((1, d)), _const_spec((1, d)),
                  _const_spec(wk.shape), _const_spec(wvt.shape), _const_spec(wqb.shape),
                  _const_spec(wzb.shape), _const_spec((1, LANES)), _const_spec((1, LANES))],
        out_specs=[tok(),
                   pl.BlockSpec((nr, B_HEADS, TM, LANES), lambda b, t: (b, 0, t, 0)),
                   pl.BlockSpec((nr, TM, LANES), lambda b, t: (b, t, 0)),
                   pl.BlockSpec((nr, B_KV_HEADS, B_HEAD_DIM, TM), lambda b, t: (b, 0, 0, t)),
                   tok()],
        out_shape=[jax.ShapeDtypeStruct((bsz, seq, d), F32),
                   jax.ShapeDtypeStruct((bsz, B_HEADS, seq, LANES), BF16),
                   jax.ShapeDtypeStruct((bsz, seq, LANES), BF16),
                   jax.ShapeDtypeStruct((bsz, B_KV_HEADS, B_HEAD_DIM, seq), BF16),
                   jax.ShapeDtypeStruct((bsz, seq, d), BF16)],
        compiler_params=_params(),
        name="mid_proj",
    )(x, og, mod, wo, gkv, gb, wk, wvt, wqb, wzb, gks, gqb)


def _final_kernel(bound_ref, h1_ref, qb_ref, ksc_ref, ksp_ref, vtc_ref, vtp_ref, zb_ref, wo_ref, mod_ref,
                  bias_ref, sink_ref, out_ref, o_scr):
    t = pl.program_id(1)
    tm = h1_ref.shape[1]
    w = WINDOW
    first_tile = t == 0
    prev_rows = lax.broadcasted_iota(jnp.int32, (2 * w, B_GROUP * w), 0) < w
    bound = bound_ref[0]

    def attend(bounded):
        if bounded:
            ref = [jnp.maximum(sink_ref[g], bound) for g in range(B_KV_HEADS)]
            ref_terms = ([bias_ref[g] - ref[g] for g in range(B_KV_HEADS)],
                         [jnp.exp2(sink_ref[g] - ref[g]) for g in range(B_KV_HEADS)])
        for bb in range(h1_ref.shape[0]):
            attend_row(bb, bounded, ref_terms if bounded else None)

    def attend_row(bb, bounded, ref_terms):
        if bounded:
            bias_rel, sink_term = ref_terms
        for nb in range(tm // w):
            if nb == 0:
                kband = jnp.concatenate([ksp_ref[bb], ksc_ref[bb, 0:w, :]], axis=0)
            else:
                kband = ksc_ref[bb, (nb - 1) * w:(nb + 1) * w, :]
            for g in range(B_KV_HEADS):
                if nb == 0:
                    vband = jnp.concatenate([vtp_ref[bb, g], vtc_ref[bb, g, :, 0:w]], axis=1)
                else:
                    vband = vtc_ref[bb, g, :, (nb - 1) * w:(nb + 1) * w]
                qs = jnp.concatenate([qb_ref[bb, B_GROUP * g + hh, nb * w:(nb + 1) * w, :]
                                      for hh in range(B_GROUP)], axis=0)
                s = lax.dot_general(kband, qs, NT_DIMS, preferred_element_type=F32)
                before_start = first_tile & prev_rows
                if bounded:
                    e = jnp.exp2(s + bias_rel[g])
                    if nb == 0:
                        e = jnp.where(before_start, 0.0, e)
                    den = jnp.sum(e, axis=0, keepdims=True) + sink_term[g]
                else:
                    s = s + bias_ref[g]
                    if nb == 0:
                        s = jnp.where(before_start, NEG, s)
                    sink = sink_ref[g]
                    m = jnp.maximum(jnp.max(s, axis=0, keepdims=True), sink)
                    e = jnp.exp2(s - m)
                    den = jnp.sum(e, axis=0, keepdims=True) + jnp.exp2(sink - m)
                o_t = jnp.dot(vband, e.astype(BF16), preferred_element_type=F32) * (1.0 / den)
                for pr in range(B_GROUP // 2):
                    pair = jnp.concatenate([o_t[:, (2 * pr) * w:(2 * pr + 1) * w],
                                            o_t[:, (2 * pr + 1) * w:(2 * pr + 2) * w]], axis=0)
                    col = (B_GROUP * g + 2 * pr) * B_HEAD_DIM
                    o_scr[bb, nb * w:(nb + 1) * w, col:col + 2 * B_HEAD_DIM] = pair.T

        og = (o_scr[bb] * _silu(zb_ref[bb].astype(F32))).astype(BF16)
        out_ref[bb] = h1_ref[bb] + mod_ref[bb, 7:8, :] * jnp.dot(og, wo_ref[...], preferred_element_type=F32)

    @pl.when(bound <= BOUND_MAX)
    def _():
        attend(True)

    @pl.when(bound > BOUND_MAX)
    def _():
        attend(False)


def _final(bound, h1, qb, ks, vst, zb, wo, mod, bias, sink):
    bsz, seq, d = h1.shape
    nt = seq // TM
    nblk = TM // WINDOW
    nr = _rows_per_step(bsz)
    tok = lambda: pl.BlockSpec((nr, TM, d), lambda b, t: (b, t, 0))
    prev = lambda t: jnp.maximum(t * nblk - 1, 0)
    return pl.pallas_call(
        _final_kernel,
        grid=(bsz // nr, nt),
        in_specs=[pl.BlockSpec(memory_space=pltpu.SMEM),
                  tok(),
                  pl.BlockSpec((nr, B_HEADS, TM, LANES), lambda b, t: (b, 0, t, 0)),
                  pl.BlockSpec((nr, TM, LANES), lambda b, t: (b, t, 0)),
                  pl.BlockSpec((nr, WINDOW, LANES), lambda b, t: (b, prev(t), 0)),
                  pl.BlockSpec((nr, B_KV_HEADS, B_HEAD_DIM, TM), lambda b, t: (b, 0, 0, t)),
                  pl.BlockSpec((nr, B_KV_HEADS, B_HEAD_DIM, WINDOW), lambda b, t: (b, 0, 0, prev(t))),
                  tok(),
                  _const_spec(wo.shape),
                  pl.BlockSpec((nr, 8, d), lambda b, t: (b, 0, 0)),
                  _const_spec(bias.shape), _const_spec(sink.shape)],
        out_specs=tok(),
        out_shape=jax.ShapeDtypeStruct((bsz, seq, d), F32),
        scratch_shapes=[pltpu.VMEM((nr, TM, d), F32)],
        compiler_params=_params(),
        name="attn_b_out",
    )(bound, h1, qb, ks, ks, vst, vst, zb, wo, mod, bias, sink)


def _logit_bound(gq, gk, head_dim):
    return (head_dim * BOUND_SLACK * jnp.max(jnp.abs(gq)) * jnp.max(jnp.abs(gk))).reshape(1)


def _pair_gain(g, scale):
    return jnp.tile(g.astype(F32) * scale, 2).reshape(1, LANES)


def _swa_bias_table():
    slopes = np.asarray(_alibi_slopes_log2(B_HEADS), dtype=np.float32)
    k = np.arange(2 * WINDOW)[:, None]
    q = np.arange(WINDOW)[None, :]
    dist = q + WINDOW - k
    valid = (dist >= 0) & (dist < WINDOW)
    tab = np.empty((B_KV_HEADS, 2 * WINDOW, B_GROUP * WINDOW), np.float32)
    for g in range(B_KV_HEADS):
        for hh in range(B_GROUP):
            blk = np.where(valid, -slopes[B_GROUP * g + hh] * dist.astype(np.float32), np.float32(NEG))
            tab[g, :, hh * WINDOW:(hh + 1) * WINDOW] = blk
    return jnp.asarray(tab)


def kernel(x, c, a_norm_g, a_ada_w, a_ada_b, a_w_in, a_q_norm_g, a_k_norm_g, a_lambda_q1, a_lambda_k1,
           a_lambda_q2, a_lambda_k2, a_subln_g, a_w_out, kv_norm_g, kv_ada_w, kv_ada_b, w_kv, kv_k_norm_g,
           b_norm_g, b_ada_w, b_ada_b, b_w_in, b_q_norm_g, b_sinks, b_w_out):
    bsz, seq, d = x.shape
    assert a_norm_g.shape[0] == 1 and b_norm_g.shape[0] == 1, "one layer of each mixer"
    assert TM == TK and seq % TM == 0 and seq % TQ == 0 and TQ % TK == 0 and TM % WINDOW == 0
    a_inner = A_HEADS * 2 * A_HEAD_DIM
    b_inner = B_HEADS * B_HEAD_DIM

    mod = _modulation(c, a_ada_w[0], a_ada_b[0], kv_ada_w, kv_ada_b, b_ada_w[0], b_ada_b[0])

    w_in = a_w_in[0]
    wqk = w_in[:, 0:2 * a_inner].astype(BF16)
    wvt = w_in[:, 2 * a_inner:3 * a_inner].T.astype(BF16)
    wz = w_in[:, 3 * a_inner:].astype(BF16)
    gq = _pair_gain(a_q_norm_g[0], A_HEAD_DIM ** -0.5 * LOG2E)
    gk = _pair_gain(a_k_norm_g[0], 1.0)
    bound_a = _logit_bound(gq, gk, A_HEAD_DIM)

    qp, kp, vt, z = _proj_a(x, mod, a_norm_g[0].reshape(1, d), wqk, wz, wvt, gq, gk)

    lambda_init = 0.8 - 0.6 * math.exp(-0.3 * 0)
    row = lambda v: v.astype(F32).reshape(1, -1)
    og = _attn_a(bound_a, qp, kp, vt, z, row(a_lambda_q1[0]), row(a_lambda_k1[0]), row(a_lambda_q2[0]),
                 row(a_lambda_k2[0]), row(a_subln_g[0]), lambda_init)

    kv_w = B_KV_HEADS * B_HEAD_DIM
    wks = w_kv[:, 0:kv_w].astype(BF16)
    wvst = w_kv[:, kv_w:].T.astype(BF16)
    wqb = b_w_in[0][:, 0:b_inner].astype(BF16)
    wzb = b_w_in[0][:, b_inner:].astype(BF16)
    gks = _pair_gain(kv_k_norm_g, 1.0)
    gqb = _pair_gain(b_q_norm_g[0], B_HEAD_DIM ** -0.5 * LOG2E)

    h1, qb, ks, vst, zb = _mid(x, og, mod, a_w_out[0].astype(BF16), kv_norm_g.reshape(1, d),
                               b_norm_g[0].reshape(1, d), wks, wvst, wqb, wzb, gks, gqb)

    sink = jnp.repeat(b_sinks[0].astype(F32).reshape(B_KV_HEADS, B_GROUP) * LOG2E, WINDOW, axis=1)
    sink = sink.reshape(B_KV_HEADS, 1, B_GROUP * WINDOW)
    bound_b = _logit_bound(gqb, gks, B_HEAD_DIM)
    return _final(bound_b, h1, qb, ks, vst, zb, b_w_out[0].astype(BF16), mod, _swa_bias_table(), sink)
```

```python
import functools
import math

import numpy as np
import jax
import jax.numpy as jnp
from jax import lax
from jax.experimental import pallas as pl
from jax.experimental.pallas import tpu as pltpu

EPS = 1e-6
LANES = 128
A_HEADS = 8
A_HEAD_DIM = 64
B_HEADS = 16
B_KV_HEADS = 2
B_GROUP = B_HEADS // B_KV_HEADS
B_HEAD_DIM = 64
WINDOW = 128
NEG = -1e30
V_ROWS = 2 * A_HEAD_DIM

TM = 512
TQ = 1024
TK = 512
A_BATCH_ROWS = 2
VMEM_LIMIT = 56 * 1024 * 1024

LOG2E = 1.4426950408889634
EXP2_ZERO = 152.0
BOUND_MAX = 50.0
BOUND_SLACK = 1.01

F32 = jnp.float32
BF16 = jnp.bfloat16
NT_DIMS = (((1,), (1,)), ((), ()))


def _alibi_slopes(n_heads):
    return [2.0 ** (-8.0 * (h + 1) / n_heads) for h in range(n_heads)]


def _alibi_slopes_log2(n_heads):
    return [s * LOG2E for s in _alibi_slopes(n_heads)]


def _silu(x):
    return x * (1.0 / (1.0 + jnp.exp(-x)))


def _const_spec(shape):
    zeros = (0,) * len(shape)
    return pl.BlockSpec(shape, lambda *_: zeros, pipeline_mode=pl.Buffered(1))


def _rows_per_step(bsz):
    return A_BATCH_ROWS if bsz % A_BATCH_ROWS == 0 else 1


def _params():
    return pltpu.CompilerParams(dimension_semantics=("arbitrary",) * 2, vmem_limit_bytes=VMEM_LIMIT)


def _mod_kernel(c_ref, aw_ref, kw_ref, bw_ref, ab_ref, kb_ref, bb_ref, o_ref):
    j = pl.program_id(0)
    sc = _silu(c_ref[...])

    def emit(w_ref, b_ref):
        o_ref[...] = jnp.dot(sc, w_ref[...], precision=lax.Precision.HIGHEST,
                             preferred_element_type=F32) + b_ref[...]

    @pl.when(j < 3)
    def _():
        emit(aw_ref, ab_ref)

    @pl.when((j >= 3) & (j < 5))
    def _():
        emit(kw_ref, kb_ref)

    @pl.when(j >= 5)
    def _():
        emit(bw_ref, bb_ref)


def _modulation(c, a_w, a_b, kv_w, kv_b, b_w, b_b):
    bsz, d = c.shape
    wspec = lambda f: pl.BlockSpec((d, d), lambda j: (0, f(j)))
    bspec = lambda f: pl.BlockSpec((1, d), lambda j: (0, f(j)))
    fa = lambda j: jnp.minimum(j, 2)
    fk = lambda j: jnp.clip(j - 3, 0, 1)
    fb = lambda j: jnp.clip(j - 5, 0, 2)
    out = pl.pallas_call(
        _mod_kernel,
        grid=(8,),
        in_specs=[pl.BlockSpec((bsz, d), lambda j: (0, 0)),
                  wspec(fa), wspec(fk), wspec(fb), bspec(fa), bspec(fk), bspec(fb)],
        out_specs=pl.BlockSpec((bsz, d), lambda j: (0, j)),
        out_shape=jax.ShapeDtypeStruct((bsz, 8 * d), F32),
        compiler_params=pltpu.CompilerParams(dimension_semantics=("arbitrary",),
                                             vmem_limit_bytes=VMEM_LIMIT),
        name="ada_modulation",
    )(c, a_w, kv_w, b_w, a_b.reshape(1, -1), kv_b.reshape(1, -1), b_b.reshape(1, -1))
    return out.reshape(bsz, 8, d)


def _modulated_norm(x, g, scale, shift):
    ms = jnp.mean(x * x, axis=-1, keepdims=True)
    return (x * lax.rsqrt(ms + EPS) * g) * (1.0 + scale) + shift


def _pair_norm(x, gain2, first_half):
    y = x * x
    ss_a = jnp.sum(jnp.where(first_half, y, 0.0), axis=-1, keepdims=True)
    ss_b = jnp.sum(jnp.where(first_half, 0.0, y), axis=-1, keepdims=True)
    ms = jnp.where(first_half, ss_a, ss_b) * (2.0 / LANES)
    return x * lax.rsqrt(ms + EPS) * gain2


def _split_pair(xn, first_half, fill_a, fill_b):
    return (jnp.where(first_half, xn, fill_a),
            jnp.where(first_half, pltpu.roll(xn, LANES // 2, axis=1), fill_b))


def _proj_a_kernel(x_ref, mod_ref, g_ref, wqk_ref, wz_ref, wvt_ref, gq_ref, gk_ref,
                   qp_ref, kp_ref, vt_ref, z_ref, *, slopes):
    tm = x_ref.shape[1]
    lane = lax.broadcasted_iota(jnp.int32, (tm, LANES), 1)
    first_half = lane < A_HEAD_DIM
    col_hi = lane == A_HEAD_DIM
    col_lo = lane == A_HEAD_DIM + 1
    pos = lax.broadcasted_iota(jnp.int32, (tm, LANES), 0).astype(F32)
    q_fill = jnp.where(col_hi | col_lo, 1.0, 0.0)
    gq = gq_ref[...]
    gk = gk_ref[...]
    a_inner = A_HEADS * 2 * A_HEAD_DIM
    for bb in range(x_ref.shape[0]):
        u = _modulated_norm(x_ref[bb], g_ref[...], mod_ref[bb, 1:2, :], mod_ref[bb, 0:1, :]).astype(BF16)
        for hp in range(A_HEADS // 2):
            c0 = hp * 2 * LANES
            qq = jnp.dot(u, wqk_ref[:, c0:c0 + 2 * LANES], preferred_element_type=F32)
            kk = jnp.dot(u, wqk_ref[:, a_inner + c0:a_inner + c0 + 2 * LANES],
                         preferred_element_type=F32)
            for hl in range(2):
                h = 2 * hp + hl
                bias = pos * slopes[h]
                bias_hi = bias.astype(BF16).astype(F32)
                k_fill = jnp.where(col_hi, bias_hi, jnp.where(col_lo, bias - bias_hi, 0.0))
                qn = _pair_norm(qq[:, hl * LANES:(hl + 1) * LANES], gq, first_half)
                kn = _pair_norm(kk[:, hl * LANES:(hl + 1) * LANES], gk, first_half)
                for s, (qs, ks) in enumerate(zip(_split_pair(qn, first_half, q_fill, q_fill),
                                                 _split_pair(kn, first_half, k_fill, k_fill))):
                    qp_ref[bb, 2 * h + s] = qs.astype(BF16)
                    kp_ref[bb, 2 * h + s] = ks.astype(BF16)

        z_ref[bb] = jnp.dot(u, wz_ref[...], preferred_element_type=F32).astype(BF16)

        v_t = lax.dot_general(wvt_ref[...], u, NT_DIMS, preferred_element_type=F32)
        for h in range(A_HEADS):
            vt_ref[bb, h, 0] = v_t[h * V_ROWS:(h + 1) * V_ROWS].astype(BF16)


def _proj_a(x, mod, g, wqk, wz, wvt, gq, gk):
    bsz, seq, d = x.shape
    nt = seq // TM
    n_sub = 2 * A_HEADS
    kern = functools.partial(_proj_a_kernel, slopes=_alibi_slopes_log2(A_HEADS))
    nr = _rows_per_step(bsz)
    return pl.pallas_call(
        kern,
        grid=(bsz // nr, nt),
        in_specs=[pl.BlockSpec((nr, TM, d), lambda b, t: (b, t, 0)),
                  pl.BlockSpec((nr, 8, d), lambda b, t: (b, 0, 0)),
                  _const_spec((1, d)),
                  _const_spec(wqk.shape), _const_spec(wz.shape), _const_spec(wvt.shape),
                  _const_spec((1, LANES)), _const_spec((1, LANES))],
        out_specs=[pl.BlockSpec((nr, n_sub, TM, LANES), lambda b, t: (b, 0, t, 0)),
                   pl.BlockSpec((nr, n_sub, TM, LANES), lambda b, t: (b, 0, t, 0)),
                   pl.BlockSpec((nr, A_HEADS, 1, V_ROWS, TM), lambda b, t: (b, 0, t, 0, 0)),
                   pl.BlockSpec((nr, TM, d), lambda b, t: (b, t, 0))],
        out_shape=[jax.ShapeDtypeStruct((bsz, n_sub, seq, LANES), BF16),
                   jax.ShapeDtypeStruct((bsz, n_sub, seq, LANES), BF16),
                   jax.ShapeDtypeStruct((bsz, A_HEADS, nt, V_ROWS, TM), BF16),
                   jax.ShapeDtypeStruct((bsz, seq, d), BF16)],
        compiler_params=_params(),
        name="proj_a",
    )(x, mod, g, wqk, wz, wvt, gq, gk)


def _attn_a_kernel(bound_ref, qp_ref, kp_ref, vt_ref, lq1_ref, lk1_ref, lq2_ref, lk2_ref, gs_ref,
                   og_ref, acc_ref, l_ref, m_ref, p_ref, *, slopes2, lambda_init):
    h = pl.program_id(1)
    i = pl.program_id(2)
    nb = qp_ref.shape[0]
    tq = qp_ref.shape[2]
    tk = kp_ref.shape[3]
    vd = 2 * A_HEAD_DIM

    slope2 = jnp.float32(slopes2[0])
    zero_dist = jnp.int32(math.ceil(EXP2_ZERO / slopes2[0]))
    for hh in range(1, A_HEADS):
        slope2 = jnp.where(h == hh, jnp.float32(slopes2[hh]), slope2)
        zero_dist = jnp.where(h == hh, jnp.int32(math.ceil(EXP2_ZERO / slopes2[hh])), zero_dist)

    bound = bound_ref[0]

    def tile_offset(j):
        return slope2 * (j * tk - i * tq).astype(F32)

    n_diag = tq // tk

    def causal_mask(width):
        krow = lax.broadcasted_iota(jnp.int32, (tk, width), 0)
        qcol = lax.broadcasted_iota(jnp.int32, (tk, width), 1)
        return krow <= qcol

    @pl.when(bound <= BOUND_MAX)
    def _():
        q_local = lax.broadcasted_iota(jnp.int32, (1, tq), 1).astype(F32)
        ref = bound + slope2 * q_local

        def probs(j, bb, sub, q0):
            s = lax.dot_general(kp_ref[bb, sub, j], qp_ref[bb, sub, q0:, :], NT_DIMS,
                                preferred_element_type=F32)
            return jnp.exp2(s + (tile_offset(j) - ref[:, q0:]))

        def stage_a(j):
            for bb in range(nb):
                for sub in range(2):
                    e = probs(j, bb, sub, 0)
                    l_ref[bb, sub] += jnp.sum(e, axis=0, keepdims=True)
                    p_ref[bb, sub] = e.astype(BF16)

        def stage_b(j):
            for bb in range(nb):
                v_t = vt_ref[bb, 0, j]
                for sub in range(2):
                    acc_ref[bb, sub] += jnp.dot(v_t, p_ref[bb, sub], preferred_element_type=F32)

        def diag_tile(r):
            q0 = r * tk
            j = i * n_diag + r
            for bb in range(nb):
                v_t = vt_ref[bb, 0, j]
                for sub in range(2):
                    e = jnp.where(causal_mask(tq - q0), probs(j, bb, sub, q0), 0.0)
                    l_ref[bb, sub, :, q0:] += jnp.sum(e, axis=0, keepdims=True)
                    acc_ref[bb, sub, :, q0:] += jnp.dot(v_t, e.astype(BF16), preferred_element_type=F32)

        first = jnp.maximum(i * tq - zero_dist + 1, 0) // tk
        last = i * n_diag

        def zero_accumulators():
            acc_ref[...] = jnp.zeros(acc_ref.shape, F32)
            l_ref[...] = jnp.zeros(l_ref.shape, F32)

        n_off = last - first
        odd = (n_off >= 2) & (n_off % 2 == 0)

        @pl.when((n_off >= 1) & jnp.logical_not(odd))
        def _():
            zero_accumulators()
            stage_a(first)

        @pl.when(odd)
        def _():
            zero_accumulators()
            stage_a(first)
            stage_b(first)
            stage_a(first + 1)

        @pl.when(n_off <= 0)
        def _():
            zero_accumulators()
            p_ref[...] = jnp.zeros(p_ref.shape, BF16)

        start = first + 1 + odd.astype(jnp.int32)

        def body(t, carry):
            j = start + 2 * t
            stage_b(j - 1)
            stage_a(j)
            stage_b(j)
            stage_a(j + 1)
            return carry

        lax.fori_loop(0, jnp.maximum(last - start, 0) // 2, body, 0)
        stage_b(jnp.maximum(last - 1, 0))
        for r in range(n_diag):
            diag_tile(r)

    @pl.when(bound > BOUND_MAX)
    def _():
        acc_ref[...] = jnp.zeros(acc_ref.shape, F32)
        l_ref[...] = jnp.zeros(l_ref.shape, F32)
        m_ref[...] = jnp.full(m_ref.shape, NEG, F32)

        def kv_tile(j, diag=None):
            q0 = 0 if diag is None else diag * tk
            c = tile_offset(j)
            for bb in range(nb):
                v_t = vt_ref[bb, 0, j]
                for sub in range(2):
                    s = lax.dot_general(kp_ref[bb, sub, j], qp_ref[bb, sub, q0:, :], NT_DIMS,
                                        preferred_element_type=F32)
                    if diag is not None:
                        s = jnp.where(causal_mask(tq - q0), s, NEG)
                    m_old = m_ref[bb, sub, :, q0:]
                    m_new = jnp.maximum(m_old, jnp.max(s, axis=0, keepdims=True) + c)
                    alpha = jnp.exp2(m_old - m_new)
                    p = jnp.exp2(s - (m_new - c))
                    pv = jnp.dot(v_t, p.astype(BF16), preferred_element_type=F32)
                    acc_ref[bb, sub, :, q0:] = acc_ref[bb, sub, :, q0:] * alpha + pv
                    l_ref[bb, sub, :, q0:] = (l_ref[bb, sub, :, q0:] * alpha
                                              + jnp.sum(p, axis=0, keepdims=True))
                    m_ref[bb, sub, :, q0:] = m_new

        def body(j, carry):
            kv_tile(j)
            return carry

        lax.fori_loop(0, i * n_diag, body, 0)
        for r in range(n_diag):
            kv_tile(i * n_diag + r, diag=r)

    lam = (jnp.exp(jnp.sum(lq1_ref[...] * lk1_ref[...], axis=-1, keepdims=True))
           - jnp.exp(jnp.sum(lq2_ref[...] * lk2_ref[...], axis=-1, keepdims=True))
           + lambda_init)
    for bb in range(nb):
        o_t = (acc_ref[bb, 0] * (1.0 / l_ref[bb, 0])
               - lam * (acc_ref[bb, 1] * (1.0 / l_ref[bb, 1])))
        ms = jnp.mean(o_t * o_t, axis=0, keepdims=True)
        o = (o_t * lax.rsqrt(ms + EPS)).T
        o = o * gs_ref[...] * (1.0 - lambda_init)
        og_ref[bb] = o.astype(BF16)


def _attn_a(bound, qp, kp, vt, z, lq1, lk1, lq2, lk2, gs, lambda_init):
    bsz, n_sub, seq, _ = qp.shape
    nk = seq // TK
    nq = seq // TQ
    d = z.shape[-1]
    vd = 2 * A_HEAD_DIM
    kp5 = kp.reshape(bsz, n_sub, nk, TK, LANES)
    nb = A_BATCH_ROWS if bsz % A_BATCH_ROWS == 0 else 1
    kern = functools.partial(_attn_a_kernel, slopes2=_alibi_slopes_log2(A_HEADS), lambda_init=lambda_init)
    vec = lambda n: pl.BlockSpec((1, n), lambda b, h, i: (0, 0))
    return pl.pallas_call(
        kern,
        grid=(bsz // nb, A_HEADS, nq),
        in_specs=[pl.BlockSpec(memory_space=pltpu.SMEM),
                  pl.BlockSpec((nb, 2, TQ, LANES), lambda b, h, i: (b, h, i, 0)),
                  pl.BlockSpec((nb, 2, nk, TK, LANES), lambda b, h, i: (b, h, 0, 0, 0)),
                  pl.BlockSpec((nb, 1, nk, V_ROWS, TK), lambda b, h, i: (b, h, 0, 0, 0)),
                  vec(A_HEAD_DIM), vec(A_HEAD_DIM), vec(A_HEAD_DIM), vec(A_HEAD_DIM), vec(vd)],
        out_specs=pl.BlockSpec((nb, TQ, vd), lambda b, h, i: (b, i, h)),
        out_shape=jax.ShapeDtypeStruct((bsz, seq, d), BF16),
        scratch_shapes=[pltpu.VMEM((nb, 2, V_ROWS, TQ), F32), pltpu.VMEM((nb, 2, 1, TQ), F32),
                        pltpu.VMEM((nb, 2, 1, TQ), F32), pltpu.VMEM((nb, 2, TK, TQ), BF16)],
        compiler_params=pltpu.CompilerParams(dimension_semantics=("arbitrary",) * 3,
                                             vmem_limit_bytes=VMEM_LIMIT),
        name="attn_a",
    )(bound, qp, kp5, vt, lq1, lk1, lq2, lk2, gs)


def _mid_kernel(x_ref, o_ref, z_ref, mod_ref, wo_ref, gkv_ref, gb_ref, wk_ref, wvt_ref, wqb_ref, wzb_ref,
                gks_ref, gqb_ref, h1_ref, qb_ref, ks_ref, vst_ref, zb_ref):
    tm = x_ref.shape[1]
    first_half = lax.broadcasted_iota(jnp.int32, (tm, LANES), 1) < B_HEAD_DIM
    gqb = gqb_ref[...]
    for bb in range(x_ref.shape[0]):
        gate = mod_ref[bb, 2:3, :]
        og = (o_ref[bb].astype(F32) * _silu(z_ref[bb].astype(F32))).astype(BF16)
        h1 = x_ref[bb] + gate * jnp.dot(og, wo_ref[...], preferred_element_type=F32)
        h1_ref[bb] = h1

        ms = jnp.mean(h1 * h1, axis=-1, keepdims=True)
        r = h1 * lax.rsqrt(ms + EPS)
        ukv = ((r * gkv_ref[...]) * (1.0 + mod_ref[bb, 4:5, :]) + mod_ref[bb, 3:4, :]).astype(BF16)
        ub = ((r * gb_ref[...]) * (1.0 + mod_ref[bb, 6:7, :]) + mod_ref[bb, 5:6, :]).astype(BF16)

        kk = jnp.dot(ukv, wk_ref[...], preferred_element_type=F32)
        ks_ref[bb] = _pair_norm(kk, gks_ref[...], first_half).astype(BF16)
        v_t = lax.dot_general(wvt_ref[...], ukv, NT_DIMS, preferred_element_type=F32)
        for g in range(B_KV_HEADS):
            vst_ref[bb, g] = v_t[g * B_HEAD_DIM:(g + 1) * B_HEAD_DIM].astype(BF16)

        for pp in range(B_HEADS // 4):
            qq = jnp.dot(ub, wqb_ref[:, pp * 2 * LANES:(pp + 1) * 2 * LANES], preferred_element_type=F32)
            for hl in range(2):
                h0 = 4 * pp + 2 * hl
                qn = _pair_norm(qq[:, hl * LANES:(hl + 1) * LANES], gqb, first_half)
                swapped = pltpu.roll(qn, LANES // 2, axis=1)
                if h0 < B_GROUP:
                    qb_ref[bb, h0] = jnp.where(first_half, qn, 0.0).astype(BF16)
                    qb_ref[bb, h0 + 1] = jnp.where(first_half, swapped, 0.0).astype(BF16)
                else:
                    qb_ref[bb, h0] = jnp.where(first_half, 0.0, swapped).astype(BF16)
                    qb_ref[bb, h0 + 1] = jnp.where(first_half, 0.0, qn).astype(BF16)
        zb_ref[bb] = jnp.dot(ub, wzb_ref[...], preferred_element_type=F32).astype(BF16)


def _mid(x, og, z, mod, wo, gkv, gb, wk, wvt, wqb, wzb, gks, gqb):
    bsz, seq, d = x.shape
    nt = seq // TM
    nr = _rows_per_step(bsz)
    tok = lambda: pl.BlockSpec((nr, TM, d), lambda b, t: (b, t, 0))
    return pl.pallas_call(
        _mid_kernel,
        grid=(bsz // nr, nt),
        in_specs=[tok(), tok(), tok(),
                  pl.BlockSpec((nr, 8, d), lambda b, t: (b, 0, 0)),
                  _const_spec(wo.shape), _const_spec((1, d)), _const_spec((1, d)),
                  _const_spec(wk.shape), _const_spec(wvt.shape), _const_spec(wqb.shape),
                  _const_spec(wzb.shape), _const_spec((1, LANES)), _const_spec((1, LANES))],
        out_specs=[tok(),
                   pl.BlockSpec((nr, B_HEADS, TM, LANES), lambda b, t: (b, 0, t, 0)),
                   pl.BlockSpec((nr, TM, LANES), lambda b, t: (b, t, 0)),
                   pl.BlockSpec((nr, B_KV_HEADS, B_HEAD_DIM, TM), lambda b, t: (b, 0, 0, t)),
                   tok()],
        out_shape=[jax.ShapeDtypeStruct((bsz, seq, d), F32),
                   jax.ShapeDtypeStruct((bsz, B_HEADS, seq, LANES), BF16),
                   jax.ShapeDtypeStruct((bsz, seq, LANES), BF16),
                   jax.ShapeDtypeStruct((bsz, B_KV_HEADS, B_HEAD_DIM, seq), BF16),
                   jax.ShapeDtypeStruct((bsz, seq, d), BF16)],
        compiler_params=_params(),
        name="mid_proj",
    )(x, og, z, mod, wo, gkv, gb, wk, wvt, wqb, wzb, gks, gqb)


def _final_kernel(bound_ref, h1_ref, qb_ref, ksc_ref, ksp_ref, vtc_ref, vtp_ref, zb_ref, wo_ref, mod_ref,
                  bias_ref, sink_ref, out_ref, o_scr):
    t = pl.program_id(1)
    tm = h1_ref.shape[1]
    w = WINDOW
    first_tile = t == 0
    prev_rows = lax.broadcasted_iota(jnp.int32, (2 * w, B_GROUP * w), 0) < w
    bound = bound_ref[0]

    def attend(bounded):
        if bounded:
            ref = [jnp.maximum(sink_ref[g], bound) for g in range(B_KV_HEADS)]
            ref_terms = ([bias_ref[g] - ref[g] for g in range(B_KV_HEADS)],
                         [jnp.exp2(sink_ref[g] - ref[g]) for g in range(B_KV_HEADS)])
        for bb in range(h1_ref.shape[0]):
            attend_row(bb, bounded, ref_terms if bounded else None)

    def attend_row(bb, bounded, ref_terms):
        if bounded:
            bias_rel, sink_term = ref_terms
        for nb in range(tm // w):
            if nb == 0:
                kband = jnp.concatenate([ksp_ref[bb], ksc_ref[bb, 0:w, :]], axis=0)
            else:
                kband = ksc_ref[bb, (nb - 1) * w:(nb + 1) * w, :]
            for g in range(B_KV_HEADS):
                if nb == 0:
                    vband = jnp.concatenate([vtp_ref[bb, g], vtc_ref[bb, g, :, 0:w]], axis=1)
                else:
                    vband = vtc_ref[bb, g, :, (nb - 1) * w:(nb + 1) * w]
                qs = jnp.concatenate([qb_ref[bb, B_GROUP * g + hh, nb * w:(nb + 1) * w, :]
                                      for hh in range(B_GROUP)], axis=0)
                s = lax.dot_general(kband, qs, NT_DIMS, preferred_element_type=F32)
                before_start = first_tile & prev_rows
                if bounded:
                    e = jnp.exp2(s + bias_rel[g])
                    if nb == 0:
                        e = jnp.where(before_start, 0.0, e)
                    den = jnp.sum(e, axis=0, keepdims=True) + sink_term[g]
                else:
                    s = s + bias_ref[g]
                    if nb == 0:
                        s = jnp.where(before_start, NEG, s)
                    sink = sink_ref[g]
                    m = jnp.maximum(jnp.max(s, axis=0, keepdims=True), sink)
                    e = jnp.exp2(s - m)
                    den = jnp.sum(e, axis=0, keepdims=True) + jnp.exp2(sink - m)
                o_t = jnp.dot(vband, e.astype(BF16), preferred_element_type=F32) * (1.0 / den)
                for pr in range(B_GROUP // 2):
                    pair = jnp.concatenate([o_t[:, (2 * pr) * w:(2 * pr + 1) * w],
                                            o_t[:, (2 * pr + 1) * w:(2 * pr + 2) * w]], axis=0)
                    col = (B_GROUP * g + 2 * pr) * B_HEAD_DIM
                    o_scr[bb, nb * w:(nb + 1) * w, col:col + 2 * B_HEAD_DIM] = pair.T

        og = (o_scr[bb] * _silu(zb_ref[bb].astype(F32))).astype(BF16)
        out_ref[bb] = h1_ref[bb] + mod_ref[bb, 7:8, :] * jnp.dot(og, wo_ref[...], preferred_element_type=F32)

    @pl.when(bound <= BOUND_MAX)
    def _():
        attend(True)

    @pl.when(bound > BOUND_MAX)
    def _():
        attend(False)


def _final(bound, h1, qb, ks, vst, zb, wo, mod, bias, sink):
    bsz, seq, d = h1.shape
    nt = seq // TM
    nblk = TM // WINDOW
    nr = _rows_per_step(bsz)
    tok = lambda: pl.BlockSpec((nr, TM, d), lambda b, t: (b, t, 0))
    prev = lambda t: jnp.maximum(t * nblk - 1, 0)
    return pl.pallas_call(
        _final_kernel,
        grid=(bsz // nr, nt),
        in_specs=[pl.BlockSpec(memory_space=pltpu.SMEM),
                  tok(),
                  pl.BlockSpec((nr, B_HEADS, TM, LANES), lambda b, t: (b, 0, t, 0)),
                  pl.BlockSpec((nr, TM, LANES), lambda b, t: (b, t, 0)),
                  pl.BlockSpec((nr, WINDOW, LANES), lambda b, t: (b, prev(t), 0)),
                  pl.BlockSpec((nr, B_KV_HEADS, B_HEAD_DIM, TM), lambda b, t: (b, 0, 0, t)),
                  pl.BlockSpec((nr, B_KV_HEADS, B_HEAD_DIM, WINDOW), lambda b, t: (b, 0, 0, prev(t))),
                  tok(),
                  _const_spec(wo.shape),
                  pl.BlockSpec((nr, 8, d), lambda b, t: (b, 0, 0)),
                  _const_spec(bias.shape), _const_spec(sink.shape)],
        out_specs=tok(),
        out_shape=jax.ShapeDtypeStruct((bsz, seq, d), F32),
        scratch_shapes=[pltpu.VMEM((nr, TM, d), F32)],
        compiler_params=_params(),
        name="attn_b_out",
    )(bound, h1, qb, ks, ks, vst, vst, zb, wo, mod, bias, sink)


def _logit_bound(gq, gk, head_dim):
    return (head_dim * BOUND_SLACK * jnp.max(jnp.abs(gq)) * jnp.max(jnp.abs(gk))).reshape(1)


def _pair_gain(g, scale):
    return jnp.tile(g.astype(F32) * scale, 2).reshape(1, LANES)


def _swa_bias_table():
    slopes = np.asarray(_alibi_slopes_log2(B_HEADS), dtype=np.float32)
    k = np.arange(2 * WINDOW)[:, None]
    q = np.arange(WINDOW)[None, :]
    dist = q + WINDOW - k
    valid = (dist >= 0) & (dist < WINDOW)
    tab = np.empty((B_KV_HEADS, 2 * WINDOW, B_GROUP * WINDOW), np.float32)
    for g in range(B_KV_HEADS):
        for hh in range(B_GROUP):
            blk = np.where(valid, -slopes[B_GROUP * g + hh] * dist.astype(np.float32), np.float32(NEG))
            tab[g, :, hh * WINDOW:(hh + 1) * WINDOW] = blk
    return jnp.asarray(tab)


def kernel(x, c, a_norm_g, a_ada_w, a_ada_b, a_w_in, a_q_norm_g, a_k_norm_g, a_lambda_q1, a_lambda_k1,
           a_lambda_q2, a_lambda_k2, a_subln_g, a_w_out, kv_norm_g, kv_ada_w, kv_ada_b, w_kv, kv_k_norm_g,
           b_norm_g, b_ada_w, b_ada_b, b_w_in, b_q_norm_g, b_sinks, b_w_out):
    bsz, seq, d = x.shape
    assert a_norm_g.shape[0] == 1 and b_norm_g.shape[0] == 1, "one layer of each mixer"
    assert TM == TK and seq % TM == 0 and seq % TQ == 0 and TQ % TK == 0 and TM % WINDOW == 0
    a_inner = A_HEADS * 2 * A_HEAD_DIM
    b_inner = B_HEADS * B_HEAD_DIM

    mod = _modulation(c, a_ada_w[0], a_ada_b[0], kv_ada_w, kv_ada_b, b_ada_w[0], b_ada_b[0])

    w_in = a_w_in[0]
    wqk = w_in[:, 0:2 * a_inner].astype(BF16)
    wvt = w_in[:, 2 * a_inner:3 * a_inner].T.astype(BF16)
    wz = w_in[:, 3 * a_inner:].astype(BF16)
    gq = _pair_gain(a_q_norm_g[0], A_HEAD_DIM ** -0.5 * LOG2E)
    gk = _pair_gain(a_k_norm_g[0], 1.0)
    bound_a = _logit_bound(gq, gk, A_HEAD_DIM)

    qp, kp, vt, z = _proj_a(x, mod, a_norm_g[0].reshape(1, d), wqk, wz, wvt, gq, gk)

    lambda_init = 0.8 - 0.6 * math.exp(-0.3 * 0)
    row = lambda v: v.astype(F32).reshape(1, -1)
    og = _attn_a(bound_a, qp, kp, vt, z, row(a_lambda_q1[0]), row(a_lambda_k1[0]), row(a_lambda_q2[0]),
                 row(a_lambda_k2[0]), row(a_subln_g[0]), lambda_init)

    kv_w = B_KV_HEADS * B_HEAD_DIM
    wks = w_kv[:, 0:kv_w].astype(BF16)
    wvst = w_kv[:, kv_w:].T.astype(BF16)
    wqb = b_w_in[0][:, 0:b_inner].astype(BF16)
    wzb = b_w_in[0][:, b_inner:].astype(BF16)
    gks = _pair_gain(kv_k_norm_g, 1.0)
    gqb = _pair_gain(b_q_norm_g[0], B_HEAD_DIM ** -0.5 * LOG2E)

    h1, qb, ks, vst, zb = _mid(x, og, z, mod, a_w_out[0].astype(BF16), kv_norm_g.reshape(1, d),
                               b_norm_g[0].reshape(1, d), wks, wvst, wqb, wzb, gks, gqb)

    sink = jnp.repeat(b_sinks[0].astype(F32).reshape(B_KV_HEADS, B_GROUP) * LOG2E, WINDOW, axis=1)
    sink = sink.reshape(B_KV_HEADS, 1, B_GROUP * WINDOW)
    bound_b = _logit_bound(gqb, gks, B_HEAD_DIM)
    return _final(bound_b, h1, qb, ks, vst, zb, b_w_out[0].astype(BF16), mod, _swa_bias_table(), sink)
```
